```python
import math
import jax, jax.numpy as jnp
from jax import lax
import numpy as np

D_MODEL = 1024
BATCH = 2
SEQ = 16384
DEPTH = 2

D_SC = 512
SC_KERNEL = 3
D_POOL = 512
POOL_WINDOWS = (2, 4, 8, 16)
N_POOL_GROUPS = 4
POOL_GROUP = D_POOL // N_POOL_GROUPS
POOL_OUT_GROUP = D_MODEL // N_POOL_GROUPS
D_CONF = 512
CONF_KERNEL = 31
N_DIFF_HEADS = 4
DIFF_HEAD_DIM = 64
D_DIFF_QK = N_DIFF_HEADS * 2 * DIFF_HEAD_DIM
D_DIFF_V = N_DIFF_HEADS * 2 * DIFF_HEAD_DIM
Q_BLOCK = 128
N_BRANCHES = 4
N_IN = 3 * D_SC + D_POOL + 2 * D_CONF + 2 * D_DIFF_QK + D_DIFF_V + N_BRANCHES * D_MODEL
D_FF = 2816
FFN_KERNEL = 3
EPS = 1e-6

kernel_name = "hybrid_parallel_gated_mixers"


def rms_norm(x, g):
    xf = x.astype(jnp.float32)
    y = xf * lax.rsqrt(jnp.mean(xf * xf, axis=-1, keepdims=True) + EPS)
    return (y * g.astype(jnp.float32)).astype(x.dtype)


def layer_norm(x, g, b):
    xf = x.astype(jnp.float32)
    mu = jnp.mean(xf, axis=-1, keepdims=True)
    var = jnp.mean(jnp.square(xf - mu), axis=-1, keepdims=True)
    y = (xf - mu) * lax.rsqrt(var + EPS)
    return (y * g.astype(jnp.float32) + b.astype(jnp.float32)).astype(x.dtype)


def causal_dwconv(x, w):
    ksz, ch = w.shape
    rhs = w.astype(x.dtype)[:, None, :]
    return lax.conv_general_dilated(
        x, rhs, window_strides=(1,), padding=[(ksz - 1, 0)],
        dimension_numbers=("NWC", "WIO", "NWC"), feature_group_count=ch)


def pool_mixer(xp, w_pool, scale):
    b, s, _ = xp.shape
    xg = xp.astype(jnp.float32).reshape(b, s, N_POOL_GROUPS, POOL_GROUP)
    cs = jnp.cumsum(xg, axis=1)
    t = jnp.arange(s)
    outs = []
    for g, w in enumerate(POOL_WINDOWS):
        csg = cs[:, :, g]
        lag = jnp.pad(csg, ((0, 0), (w, 0), (0, 0)))[:, :s]
        count = jnp.minimum(t + 1, w).astype(jnp.float32)[None, :, None]
        outs.append((csg - lag) / count - xg[:, :, g])
    pooled = jnp.stack(outs, axis=2).astype(xp.dtype)
    y = jnp.einsum("bsgc,gco->bsgo", pooled, w_pool.astype(xp.dtype))
    return y.reshape(b, s, D_MODEL) * scale.astype(xp.dtype)


def conformer_conv(u, conv_w, conv_b, ln_g, ln_b, w_out):
    a, gate = jnp.split(u, 2, axis=-1)
    y = a * jax.nn.sigmoid(gate)
    y = causal_dwconv(y, conv_w) + conv_b.astype(y.dtype)
    y = layer_norm(y, ln_g, ln_b)
    y = jax.nn.silu(y)
    return y @ w_out.astype(y.dtype)


def diff_attention(q, k, v, q_norm, k_norm, lq1, lk1, lq2, lk2, subln, lam_init):
    b, s, _ = q.shape
    hd = DIFF_HEAD_DIM
    q = rms_norm(q.reshape(b, s, N_DIFF_HEADS, 2, hd), q_norm) * (hd ** -0.5)
    k = rms_norm(k.reshape(b, s, N_DIFF_HEADS, 2, hd), k_norm)
    v = v.reshape(b, s, N_DIFF_HEADS, 2 * hd)
    qt = jnp.transpose(q, (0, 2, 3, 1, 4))
    kt = jnp.transpose(k, (0, 2, 3, 1, 4))
    vt = jnp.transpose(v, (0, 2, 1, 3))
    lam = (jnp.exp(jnp.sum(lq1.astype(jnp.float32) * lk1.astype(jnp.float32)))
           - jnp.exp(jnp.sum(lq2.astype(jnp.float32) * lk2.astype(jnp.float32)))
           + lam_init)
    nb = s // Q_BLOCK
    q_blocks = jnp.moveaxis(qt.reshape(b, N_DIFF_HEADS, 2, nb, Q_BLOCK, hd), 3, 0)
    starts = jnp.arange(nb, dtype=jnp.int32) * Q_BLOCK
    kpos = jnp.arange(s, dtype=jnp.int32)

    def attend(args):
        qb, start = args
        sc = jnp.einsum("bhmqd,bhmkd->bhmqk", qb, kt).astype(jnp.float32)
        qpos = start + jnp.arange(Q_BLOCK, dtype=jnp.int32)
        mask = kpos[None, :] <= qpos[:, None]
        p = jax.nn.softmax(jnp.where(mask, sc, -jnp.inf), axis=-1)
        a = (p[:, :, 0] - lam * p[:, :, 1]).astype(vt.dtype)
        return jnp.einsum("bhqk,bhkv->bhqv", a, vt)

    o = lax.map(attend, (q_blocks, starts))
    o = jnp.transpose(o, (1, 0, 3, 2, 4)).reshape(b, s, N_DIFF_HEADS, 2 * hd)
    o = rms_norm(o, subln) * (1.0 - lam_init)
    return o.reshape(b, s, D_DIFF_V)


def conv_glu_ffn(h, w_up, conv_w, w_down):
    u = causal_dwconv(h @ w_up.astype(h.dtype), conv_w)
    gate, up = jnp.split(u, 2, axis=-1)
    return (jax.nn.silu(gate) * up) @ w_down.astype(h.dtype)


def setup_inputs(seed: int = 0) -> dict:
    key = jax.random.key(seed)
    ks = jax.random.split(key, 32)
    L = DEPTH
    f32 = jnp.float32

    def nrm(k, shape, scale):
        return jax.random.normal(k, shape, f32) * scale

    def gain(k, shape):
        return 1.0 + 0.02 * jax.random.normal(k, shape, f32)

    return {
        "x": nrm(ks[0], (BATCH, SEQ, D_MODEL), 1.0),
        "mix_norm": gain(ks[1], (L, D_MODEL)),
        "w_in": nrm(ks[2], (L, D_MODEL, N_IN), D_MODEL ** -0.5),
        "sc_conv": nrm(ks[3], (L, SC_KERNEL, D_SC), SC_KERNEL ** -0.5),
        "sc_out": nrm(ks[4], (L, D_SC, D_MODEL), D_SC ** -0.5),
        "pool_w": nrm(ks[5], (L, N_POOL_GROUPS, POOL_GROUP, POOL_OUT_GROUP), POOL_GROUP ** -0.5),
        "pool_scale": 1.0 + 0.1 * jax.random.normal(ks[6], (L, D_MODEL), f32),
        "conf_conv": nrm(ks[7], (L, CONF_KERNEL, D_CONF), CONF_KERNEL ** -0.5),
        "conf_conv_b": nrm(ks[8], (L, D_CONF), 0.02),
        "conf_ln_g": gain(ks[9], (L, D_CONF)),
        "conf_ln_b": nrm(ks[10], (L, D_CONF), 0.02),
        "conf_out": nrm(ks[11], (L, D_CONF, D_MODEL), D_CONF ** -0.5),
        "q_norm": gain(ks[12], (L, DIFF_HEAD_DIM)),
        "k_norm": gain(ks[13], (L, DIFF_HEAD_DIM)),
        "lambda_q1": nrm(ks[14], (L, DIFF_HEAD_DIM), 0.1),
        "lambda_k1": nrm(ks[15], (L, DIFF_HEAD_DIM), 0.1),
        "lambda_q2": nrm(ks[16], (L, DIFF_HEAD_DIM), 0.1),
        "lambda_k2": nrm(ks[17], (L, DIFF_HEAD_DIM), 0.1),
        "diff_subln": gain(ks[18], (L, 2 * DIFF_HEAD_DIM)),
        "diff_out": nrm(ks[19], (L, D_DIFF_V, D_MODEL), D_DIFF_V ** -0.5),
        "w_o": nrm(ks[20], (L, D_MODEL, D_MODEL), D_MODEL ** -0.5),
        "ffn_norm": gain(ks[21], (L, D_MODEL)),
        "ffn_up": nrm(ks[22], (L, D_MODEL, 2 * D_FF), D_MODEL ** -0.5),
        "ffn_conv": nrm(ks[23], (L, FFN_KERNEL, 2 * D_FF), FFN_KERNEL ** -0.5),
        "ffn_down": nrm(ks[24], (L, D_FF, D_MODEL), D_FF ** -0.5),
    }


def reference(x, mix_norm, w_in, sc_conv, sc_out, pool_w, pool_scale, conf_conv, conf_conv_b,
              conf_ln_g, conf_ln_b, conf_out, q_norm, k_norm, lambda_q1, lambda_k1, lambda_q2,
              lambda_k2, diff_subln, diff_out, w_o, ffn_norm, ffn_up, ffn_conv, ffn_down):
    b, s, _ = x.shape
    widths = [D_SC, D_SC, D_SC, D_POOL, 2 * D_CONF, D_DIFF_QK, D_DIFF_QK, D_DIFF_V,
              N_BRANCHES * D_MODEL]
    offsets = [int(o) for o in np.cumsum(widths)[:-1]]
    for l in range(DEPTH):
        lam_init = 0.8 - 0.6 * math.exp(-0.3 * l)
        h = rms_norm(x, mix_norm[l])
        z = h @ w_in[l].astype(h.dtype)
        sc_b, sc_c, sc_x, pool_in, conf_in, q, k, v, gate_logits = jnp.split(z, offsets, axis=-1)
        y_sc = (sc_b * causal_dwconv(sc_c * sc_x, sc_conv[l])) @ sc_out[l].astype(h.dtype)
        y_pool = pool_mixer(pool_in, pool_w[l], pool_scale[l])
        y_conf = conformer_conv(conf_in, conf_conv[l], conf_conv_b[l], conf_ln_g[l],
                                conf_ln_b[l], conf_out[l])
        y_diff = diff_attention(q, k, v, q_norm[l], k_norm[l], lambda_q1[l], lambda_k1[l],
                                lambda_q2[l], lambda_k2[l], diff_subln[l], lam_init)
        y_diff = y_diff @ diff_out[l].astype(h.dtype)
        gates = jax.nn.sigmoid(gate_logits.astype(jnp.float32)).reshape(b, s, N_BRANCHES, D_MODEL)
        ys = jnp.stack([y_sc, y_pool, y_conf, y_diff], axis=2)
        merged = jnp.sum(gates.astype(ys.dtype) * ys, axis=2)
        x = x + merged @ w_o[l].astype(h.dtype)
        h2 = rms_norm(x, ffn_norm[l])
        x = x + conv_glu_ffn(h2, ffn_up[l], ffn_conv[l], ffn_down[l])
    return x
```

```python
import functools
import math

import jax
import jax.numpy as jnp
from jax import lax
from jax.experimental import pallas as pl
from jax.experimental.pallas import tpu as pltpu

D_MODEL = 1024
D_SC = 512
D_POOL = 512
POOL_WINDOWS = (2, 4, 8, 16)
POOL_GROUP = 128
POOL_OUT_GROUP = 256
D_CONF = 512
CONF_KERNEL = 31
N_HEADS = 4
HEAD_DIM = 64
D_QK = 512
D_V = 512
D_FF = 2816
EPS = 1e-6
MASKED = -1e30

OFF_SC = 0
OFF_POOL = 3 * D_SC
OFF_CONF = OFF_POOL + D_POOL
OFF_Q = OFF_CONF + 2 * D_CONF
OFF_K = OFF_Q + D_QK
OFF_V = OFF_K + D_QK
OFF_GATE = OFF_V + D_V
N_IN = OFF_GATE + 4 * D_MODEL

SUBLANES = 8
LANES = 128
VMEM_LIMIT = 56 * 1024 * 1024

TM_FRONT = 512
TM_FFN = 512
TQ = 512
FF_CHUNKS = (768, 768, 768, 512)

SC_HALO = 8
POOL_HALO = 16
CONF_HALO = 32
FFN_HALO = 8

F32 = jnp.float32
BF16 = jnp.bfloat16


def _rms_norm_rows(x, gain):
    ms = jnp.mean(x * x, axis=-1, keepdims=True)
    return x * lax.rsqrt(ms + EPS) * gain


def _sigmoid(x):
    return 1.0 / (1.0 + jnp.exp(-x))


def _lagged(ext, lag):
    if lag == 0:
        return ext
    return pltpu.roll(ext, lag, axis=0)


def _causal_conv3(ext, w, halo, rows):
    y = ext * w[2:3] + _lagged(ext, 1) * w[1:2] + _lagged(ext, 2) * w[0:1]
    return y[halo:halo + rows]


def _mixer_front_kernel(x_ref, gain_ref, w_ref, scw_ref, sco_ref, poolw_ref, pools_ref,
                        cfw_ref, cfb_ref, lng_ref, lnb_ref, cfo_ref, qg_ref, kg_ref, gsum_ref,
                        q_out, k_out, v_out, pm_out, g3_out,
                        sc_halo, pool_halo, conf_halo):
    tm = TM_FRONT

    @pl.when(pl.program_id(1) == 0)
    def _():
        sc_halo[...] = jnp.zeros_like(sc_halo)
        pool_halo[...] = jnp.zeros_like(pool_halo)
        conf_halo[...] = jnp.zeros_like(conf_halo)

    x = x_ref[0]
    h = _rms_norm_rows(x, gain_ref[...]).astype(BF16)

    def proj(lo, width):
        return jnp.dot(h, w_ref[:, lo:lo + width], preferred_element_type=F32)

    def gate(i):
        return _sigmoid(proj(OFF_GATE + i * D_MODEL, D_MODEL))

    z = proj(OFF_SC, 3 * D_SC)
    sc_b = z[:, :D_SC]
    u = z[:, D_SC:2 * D_SC] * z[:, 2 * D_SC:]
    ext = jnp.concatenate([sc_halo[...], u], axis=0)
    sc_halo[...] = u[tm - SC_HALO:]
    y = sc_b * _causal_conv3(ext, scw_ref[...], SC_HALO, tm)
    y = jnp.dot(y.astype(BF16), sco_ref[...], preferred_element_type=F32)
    merged = gate(0) * y

    xp = proj(OFF_POOL, D_POOL)
    ext = jnp.concatenate([pool_halo[...], xp], axis=0)
    pool_halo[...] = xp[tm - POOL_HALO:]
    t_pos = pl.program_id(1) * tm + lax.broadcasted_iota(jnp.int32, (tm, 1), 0)
    ys = []
    for g, win in enumerate(POOL_WINDOWS):
        s = ext[:, g * POOL_GROUP:(g + 1) * POOL_GROUP]
        span = 1
        while span < win:
            s = s + _lagged(s, span)
            span *= 2
        count = jnp.minimum(t_pos + 1, win).astype(F32)
        pooled = s[POOL_HALO:] / count - xp[:, g * POOL_GROUP:(g + 1) * POOL_GROUP]
        ys.append(jnp.dot(pooled.astype(BF16), poolw_ref[g], preferred_element_type=F32))
    y = jnp.concatenate(ys, axis=1) * pools_ref[...]
    merged = merged + gate(1) * y

    z = proj(OFF_CONF, 2 * D_CONF)
    glu = z[:, :D_CONF] * _sigmoid(z[:, D_CONF:])
    ext = jnp.concatenate([conf_halo[...], glu], axis=0)
    conf_halo[...] = glu[tm - CONF_HALO:]
    cw = cfw_ref[...]
    acc = jnp.zeros((tm, D_CONF), F32) + cfb_ref[...]
    for r in range(SUBLANES):
        ext_r = _lagged(ext, r)
        for a in range(CONF_HALO // SUBLANES):
            lag = SUBLANES * a + r
            if lag >= CONF_KERNEL:
                continue
            tap = CONF_KERNEL - 1 - lag
            start = CONF_HALO - SUBLANES * a
            acc = acc + ext_r[start:start + tm] * cw[tap:tap + 1]
    mu = jnp.mean(acc, axis=-1, keepdims=True)
    cen = acc - mu
    var = jnp.mean(cen * cen, axis=-1, keepdims=True)
    y = cen * lax.rsqrt(var + EPS) * lng_ref[...] + lnb_ref[...]
    y = y * _sigmoid(y)
    y = jnp.dot(y.astype(BF16), cfo_ref[...], preferred_element_type=F32)
    merged = merged + gate(2) * y
    pm_out[0] = merged
    g3_out[0] = gate(3)

    def qk_norm(lo, gain):
        t = proj(lo, D_QK)
        ss = jnp.dot((t * t).astype(BF16), gsum_ref[...], preferred_element_type=F32)
        return (t * lax.rsqrt(ss * (1.0 / HEAD_DIM) + EPS) * gain).astype(BF16)

    q_out[0] = qk_norm(OFF_Q, qg_ref[...])
    k_out[0] = qk_norm(OFF_K, kg_ref[...])
    v_out[0] = proj(OFF_V, D_V).astype(BF16)


def _resident(shape):
    zeros = (0,) * len(shape)
    return pl.BlockSpec(shape, lambda b, i: zeros, pipeline_mode=pl.Buffered(1))


def _mixer_front(x, gain, w_in, sc_conv, sc_out, pool_w, pool_scale, conf_conv, conf_b,
                 ln_g, ln_b, conf_out, q_gain, k_gain, gsum):
    bsz, seq, _ = x.shape
    tm = TM_FRONT
    grid = (bsz, seq // tm)
    tok = lambda width: pl.BlockSpec((1, tm, width), lambda b, i: (b, i, 0))
    in_specs = [tok(D_MODEL)] + [_resident(a.shape) for a in
                                 (gain, w_in, sc_conv, sc_out, pool_w, pool_scale, conf_conv,
                                  conf_b, ln_g, ln_b, conf_out, q_gain, k_gain, gsum)]
    out_shape = [jax.ShapeDtypeStruct((bsz, seq, D_QK), BF16),
                 jax.ShapeDtypeStruct((bsz, seq, D_QK), BF16),
                 jax.ShapeDtypeStruct((bsz, seq, D_V), BF16),
                 jax.ShapeDtypeStruct((bsz, seq, D_MODEL), F32),
                 jax.ShapeDtypeStruct((bsz, seq, D_MODEL), F32)]
    out_specs = [tok(D_QK), tok(D_QK), tok(D_V), tok(D_MODEL), tok(D_MODEL)]
    return pl.pallas_call(
        _mixer_front_kernel,
        grid=grid,
        in_specs=in_specs,
        out_specs=out_specs,
        out_shape=out_shape,
        scratch_shapes=[pltpu.VMEM((SC_HALO, D_SC), F32),
                        pltpu.VMEM((POOL_HALO, D_POOL), F32),
                        pltpu.VMEM((CONF_HALO, D_CONF), F32)],
        compiler_params=pltpu.CompilerParams(
            dimension_semantics=("arbitrary", "arbitrary"),
            vmem_limit_bytes=VMEM_LIMIT),
        name="mixer_front",
    )(x, gain, w_in, sc_conv, sc_out, pool_w, pool_scale, conf_conv, conf_b,
      ln_g, ln_b, conf_out, q_gain, k_gain, gsum)


def _diff_attn_kernel(q_ref, k_ref, v_ref, lam_ref, subln_ref, o_ref, *, lam_init):
    tq = TQ
    qi = pl.program_id(2)
    q = q_ref[0]
    lane = lax.broadcasted_iota(jnp.int32, (tq, 2 * HEAD_DIM), 1)
    zero = jnp.zeros_like(q)
    qs = jnp.concatenate([jnp.where(lane < HEAD_DIM, q, zero),
                          jnp.where(lane >= HEAD_DIM, q, zero)], axis=0)

    def step(j, carry, masked):
        m_prev, l_prev, acc_prev = carry
        start = pl.multiple_of(j * tq, tq)
        k = k_ref[0, pl.ds(start, tq), :]
        v = v_ref[0, pl.ds(start, tq), :]
        s = lax.dot_general(qs, k, (((1,), (1,)), ((), ())), preferred_element_type=F32)
        if masked:
            row = lax.broadcasted_iota(jnp.int32, (2 * tq, tq), 0)
            col = lax.broadcasted_iota(jnp.int32, (2 * tq, tq), 1)
            row = jnp.where(row >= tq, row - tq, row)
            s = jnp.where(col <= row, s, MASKED)
        m_new = jnp.maximum(m_prev, jnp.max(s, axis=-1, keepdims=True))
        alpha = jnp.exp(m_prev - m_new)
        p = jnp.exp(s - m_new)
        l_new = alpha * l_prev + jnp.sum(p, axis=-1, keepdims=True)
        acc_new = alpha * acc_prev + jnp.dot(p.astype(BF16), v, preferred_element_type=F32)
        return m_new, l_new, acc_new

    init = (jnp.full((2 * tq, 1), MASKED, F32),
            jnp.zeros((2 * tq, 1), F32),
            jnp.zeros((2 * tq, 2 * HEAD_DIM), F32))
    carry = lax.fori_loop(0, qi, functools.partial(step, masked=False), init)
    _, l_fin, acc = step(qi, carry, masked=True)
    o = acc / l_fin
    lam_v = lam_ref[...]
    lam = (jnp.exp(jnp.sum(lam_v[0:1] * lam_v[1:2], axis=-1, keepdims=True))
           - jnp.exp(jnp.sum(lam_v[2:3] * lam_v[3:4], axis=-1, keepdims=True)) + lam_init)
    o = o[:tq] - lam * o[tq:]
    o = _rms_norm_rows(o, subln_ref[...]) * (1.0 - lam_init)
    o_ref[0] = o.astype(BF16)


def _diff_attn(q, k, v, lam_vecs, subln, lam_init):
    bsz, seq, _ = q.shape
    tq = TQ
    hw = 2 * HEAD_DIM
    grid = (bsz, N_HEADS, seq // tq)
    return pl.pallas_call(
        functools.partial(_diff_attn_kernel, lam_init=lam_init),
        grid=grid,
        in_specs=[pl.BlockSpec((1, tq, hw), lambda b, h, i: (b, i, h)),
                  pl.BlockSpec((1, seq, hw), lambda b, h, i: (b, 0, h)),
                  pl.BlockSpec((1, seq, hw), lambda b, h, i: (b, 0, h)),
                  pl.BlockSpec(lam_vecs.shape, lambda b, h, i: (0, 0)),
                  pl.BlockSpec(subln.shape, lambda b, h, i: (0, 0))],
        out_specs=pl.BlockSpec((1, tq, hw), lambda b, h, i: (b, i, h)),
        out_shape=jax.ShapeDtypeStruct((bsz, seq, D_V), BF16),
        compiler_params=pltpu.CompilerParams(
            dimension_semantics=("arbitrary", "arbitrary", "arbitrary"),
            vmem_limit_bytes=VMEM_LIMIT),
        name="diff_attn",
    )(q, k, v, lam_vecs, subln)


def _merge_ffn_kernel(x_ref, o_ref, pm_ref, g3_ref, wd_ref, wo_ref, gain_ref, wup_ref,
                      fcw_ref, wdown_ref, out_ref, halo):
    tm = TM_FFN

    @pl.when(pl.program_id(1) == 0)
    def _():
        halo[...] = jnp.zeros_like(halo)

    y_diff = jnp.dot(o_ref[0], wd_ref[...], preferred_element_type=F32)
    merged = pm_ref[0] + g3_ref[0] * y_diff
    x1 = x_ref[0] + jnp.dot(merged.astype(BF16), wo_ref[...], preferred_element_type=F32)
    h2 = _rms_norm_rows(x1, gain_ref[...]).astype(BF16)

    def conv_proj(lo, width):
        u = jnp.dot(h2, wup_ref[:, lo:lo + width], preferred_element_type=F32)
        ext = jnp.concatenate([halo[:, lo:lo + width], u], axis=0)
        halo[:, lo:lo + width] = u[tm - FFN_HALO:]
        return _causal_conv3(ext, fcw_ref[:, lo:lo + width], FFN_HALO, tm)

    acc = x1
    lo = 0
    for width in FF_CHUNKS:
        g = conv_proj(lo, width)
        up = conv_proj(D_FF + lo, width)
        a = (g * _sigmoid(g) * up).astype(BF16)
        acc = acc + jnp.dot(a, wdown_ref[lo:lo + width, :], preferred_element_type=F32)
        lo += width
    out_ref[0] = acc


def _merge_ffn(x, o, pm, g3, w_diff, w_o, gain, w_up, ffn_conv, w_down):
    bsz, seq, _ = x.shape
    tm = TM_FFN
    grid = (bsz, seq // tm)
    tok = lambda width: pl.BlockSpec((1, tm, width), lambda b, i: (b, i, 0))
    in_specs = [tok(D_MODEL), tok(D_V), tok(D_MODEL), tok(D_MODEL)] + [
        _resident(a.shape) for a in (w_diff, w_o, gain, w_up, ffn_conv, w_down)]
    return pl.pallas_call(
        _merge_ffn_kernel,
        grid=grid,
        in_specs=in_specs,
        out_specs=tok(D_MODEL),
        out_shape=jax.ShapeDtypeStruct((bsz, seq, D_MODEL), F32),
        scratch_shapes=[pltpu.VMEM((FFN_HALO, 2 * D_FF), F32)],
        compiler_params=pltpu.CompilerParams(
            dimension_semantics=("arbitrary", "arbitrary"),
            vmem_limit_bytes=VMEM_LIMIT),
        name="merge_ffn",
    )(x, o, pm, g3, w_diff, w_o, gain, w_up, ffn_conv, w_down)


def kernel(x, mix_norm, w_in, sc_conv, sc_out, pool_w, pool_scale, conf_conv, conf_conv_b,
           conf_ln_g, conf_ln_b, conf_out, q_norm, k_norm, lambda_q1, lambda_k1, lambda_q2,
           lambda_k2, diff_subln, diff_out, w_o, ffn_norm, ffn_up, ffn_conv, ffn_down):
    depth = w_in.shape[0]
    assert sum(FF_CHUNKS) == D_FF and x.shape[1] % max(TM_FRONT, TM_FFN, TQ) == 0
    row = lambda a: a.reshape(1, -1).astype(F32)
    group = jnp.arange(D_QK) // HEAD_DIM
    gsum = (group[:, None] == group[None, :]).astype(BF16)
    for l in range(depth):
        lam_init = 0.8 - 0.6 * math.exp(-0.3 * l)
        q_gain = row(jnp.tile(q_norm[l], D_QK // HEAD_DIM)) * (HEAD_DIM ** -0.5)
        k_gain = row(jnp.tile(k_norm[l], D_QK // HEAD_DIM))
        q, k, v, pm, g3 = _mixer_front(
            x, row(mix_norm[l]), w_in[l].astype(BF16), sc_conv[l], sc_out[l].astype(BF16),
            pool_w[l].astype(BF16), row(pool_scale[l]), conf_conv[l], row(conf_conv_b[l]),
            row(conf_ln_g[l]), row(conf_ln_b[l]), conf_out[l].astype(BF16), q_gain, k_gain, gsum)
        lam_vecs = jnp.stack([lambda_q1[l], lambda_k1[l], lambda_q2[l], lambda_k2[l]]).astype(F32)
        o = _diff_attn(q, k, v, lam_vecs, row(diff_subln[l]), lam_init)
        x = _merge_ffn(x, o, pm, g3, diff_out[l].astype(BF16), w_o[l].astype(BF16),
                       row(ffn_norm[l]), ffn_up[l].astype(BF16), ffn_conv[l],
                       ffn_down[l].astype(BF16))
    return x
```

```python
import functools
import math

import jax
import jax.numpy as jnp
from jax import lax
from jax.experimental import pallas as pl
from jax.experimental.pallas import tpu as pltpu

D_MODEL = 1024
D_SC = 512
D_POOL = 512
POOL_WINDOWS = (2, 4, 8, 16)
POOL_GROUP = 128
POOL_OUT_GROUP = 256
D_CONF = 512
CONF_KERNEL = 31
N_HEADS = 4
HEAD_DIM = 64
D_QK = 512
D_V = 512
D_FF = 2816
EPS = 1e-6
MASKED = -1e30
LOG2_E = math.log2(math.e)
MAX_FIXED_SHIFT = 60.0

OFF_SC = 0
OFF_POOL = 3 * D_SC
OFF_CONF = OFF_POOL + D_POOL
OFF_Q = OFF_CONF + 2 * D_CONF
OFF_K = OFF_Q + D_QK
OFF_V = OFF_K + D_QK
OFF_GATE = OFF_V + D_V
N_IN = OFF_GATE + 4 * D_MODEL

SUBLANES = 8
LANES = 128
VMEM_LIMIT = 56 * 1024 * 1024

TM_FRONT = 512
TM_FFN = 512
TQ = 1024
TK = 512
FF_CHUNKS = (768, 768, 768, 512)
GATE_SLICE = 512

SC_HALO = 8
POOL_HALO = 16
CONF_HALO = 32
FFN_HALO = 8

F32 = jnp.float32
BF16 = jnp.bfloat16


def _rms_norm_rows(x, gain):
    ms = jnp.mean(x * x, axis=-1, keepdims=True)
    return x * lax.rsqrt(ms + EPS) * gain


def _sigmoid(x):
    return 0.5 * jnp.tanh(0.5 * x) + 0.5


def _lagged(ext, lag):
    if lag == 0:
        return ext
    return pltpu.roll(ext, lag, axis=0)


def _causal_conv3(ext, w, halo, rows):
    y = ext * w[2:3] + _lagged(ext, 1) * w[1:2] + _lagged(ext, 2) * w[0:1]
    return y[halo:halo + rows]


def _mixer_front_kernel(x_ref, gain_ref, w_ref, scw_ref, sco_ref, poolw_ref, pools_ref,
                        cfw_ref, cfb_ref, lng_ref, lnb_ref, cfo_ref, qg_ref, kg_ref, gsum_ref,
                        q_out, k_out, v_out, pm_out, g3_out,
                        sc_halo, pool_halo, conf_halo):
    tm = TM_FRONT

    @pl.when(pl.program_id(1) == 0)
    def _():
        sc_halo[...] = jnp.zeros_like(sc_halo)
        pool_halo[...] = jnp.zeros_like(pool_halo)
        conf_halo[...] = jnp.zeros_like(conf_halo)

    x = x_ref[0]
    h = _rms_norm_rows(x, gain_ref[...]).astype(BF16)

    def proj(lo, width):
        return jnp.dot(h, w_ref[:, lo:lo + width], preferred_element_type=F32)

    def gate(i):
        return _sigmoid(proj(OFF_GATE + i * D_MODEL, D_MODEL))

    z_sc = proj(OFF_SC, 3 * D_SC)
    z_cf = proj(OFF_CONF, 2 * D_CONF)
    xp = proj(OFF_POOL, D_POOL)

    u = z_sc[:, D_SC:2 * D_SC] * z_sc[:, 2 * D_SC:]
    ext = jnp.concatenate([sc_halo[...], u], axis=0)
    sc_halo[...] = u[tm - SC_HALO:]
    a_act = (z_sc[:, :D_SC] * _causal_conv3(ext, scw_ref[...], SC_HALO, tm)).astype(BF16)

    def qk_norm(lo, gain):
        t = proj(lo, D_QK)
        ss = jnp.dot((t * t).astype(BF16), gsum_ref[...], preferred_element_type=F32)
        return (t * lax.rsqrt(ss * (1.0 / HEAD_DIM) + EPS) * gain).astype(BF16)

    q_out[0] = qk_norm(OFF_Q, qg_ref[...])
    k_out[0] = qk_norm(OFF_K, kg_ref[...])
    v_out[0] = proj(OFF_V, D_V).astype(BF16)
    merged = gate(0) * jnp.dot(a_act, sco_ref[...], preferred_element_type=F32)

    ext = jnp.concatenate([pool_halo[...], xp], axis=0)
    pool_halo[...] = xp[tm - POOL_HALO:]
    t_pos = pl.program_id(1) * tm + lax.broadcasted_iota(jnp.int32, (tm, 1), 0)
    ys = []
    for g, win in enumerate(POOL_WINDOWS):
        s = ext[:, g * POOL_GROUP:(g + 1) * POOL_GROUP]
        span = 1
        while span < win:
            s = s + _lagged(s, span)
            span *= 2
        count = jnp.minimum(t_pos + 1, win).astype(F32)
        pooled = s[POOL_HALO:] / count - xp[:, g * POOL_GROUP:(g + 1) * POOL_GROUP]
        ys.append(jnp.dot(pooled.astype(BF16), poolw_ref[g], preferred_element_type=F32))
    y_pool = jnp.concatenate(ys, axis=1) * pools_ref[...]

    glu = z_cf[:, :D_CONF] * _sigmoid(z_cf[:, D_CONF:])
    ext = jnp.concatenate([conf_halo[...], glu], axis=0)
    conf_halo[...] = glu[tm - CONF_HALO:]
    cw = cfw_ref[...]
    acc = jnp.zeros((tm, D_CONF), F32) + cfb_ref[...]
    gate_cols = []
    for r in range(SUBLANES):
        ext_r = _lagged(ext, r)
        for a in range(CONF_HALO // SUBLANES):
            lag = SUBLANES * a + r
            if lag >= CONF_KERNEL:
                continue
            tap = CONF_KERNEL - 1 - lag
            start = CONF_HALO - SUBLANES * a
            acc = acc + ext_r[start:start + tm] * cw[tap:tap + 1]
        if len(gate_cols) * GATE_SLICE < 3 * D_MODEL:
            lo = OFF_GATE + D_MODEL + len(gate_cols) * GATE_SLICE
            gate_cols.append(_sigmoid(proj(lo, GATE_SLICE)))
    mu = jnp.mean(acc, axis=-1, keepdims=True)
    cen = acc - mu
    var = jnp.mean(cen * cen, axis=-1, keepdims=True)
    y = cen * lax.rsqrt(var + EPS) * lng_ref[...] + lnb_ref[...]
    c_act = (y * _sigmoid(y)).astype(BF16)

    gates = jnp.concatenate(gate_cols, axis=1)
    g3_out[0] = gates[:, 2 * D_MODEL:]
    merged = merged + gates[:, :D_MODEL] * y_pool
    y_conf = jnp.dot(c_act, cfo_ref[...], preferred_element_type=F32)
    pm_out[0] = merged + gates[:, D_MODEL:2 * D_MODEL] * y_conf


def _resident(shape):
    zeros = (0,) * len(shape)
    return pl.BlockSpec(shape, lambda b, i: zeros, pipeline_mode=pl.Buffered(1))


def _mixer_front(x, gain, w_in, sc_conv, sc_out, pool_w, pool_scale, conf_conv, conf_b,
                 ln_g, ln_b, conf_out, q_gain, k_gain, gsum):
    bsz, seq, _ = x.shape
    tm = TM_FRONT
    grid = (bsz, seq // tm)
    tok = lambda width: pl.BlockSpec((1, tm, width), lambda b, i: (b, i, 0))
    in_specs = [tok(D_MODEL)] + [_resident(a.shape) for a in
                                 (gain, w_in, sc_conv, sc_out, pool_w, pool_scale, conf_conv,
                                  conf_b, ln_g, ln_b, conf_out, q_gain, k_gain, gsum)]
    out_shape = [jax.ShapeDtypeStruct((bsz, seq, D_QK), BF16),
                 jax.ShapeDtypeStruct((bsz, seq, D_QK), BF16),
                 jax.ShapeDtypeStruct((bsz, seq, D_V), BF16),
                 jax.ShapeDtypeStruct((bsz, seq, D_MODEL), F32),
                 jax.ShapeDtypeStruct((bsz, seq, D_MODEL), F32)]
    out_specs = [tok(D_QK), tok(D_QK), tok(D_V), tok(D_MODEL), tok(D_MODEL)]
    return pl.pallas_call(
        _mixer_front_kernel,
        grid=grid,
        in_specs=in_specs,
        out_specs=out_specs,
        out_shape=out_shape,
        scratch_shapes=[pltpu.VMEM((SC_HALO, D_SC), F32),
                        pltpu.VMEM((POOL_HALO, D_POOL), F32),
                        pltpu.VMEM((CONF_HALO, D_CONF), F32)],
        compiler_params=pltpu.CompilerParams(
            dimension_semantics=("arbitrary", "arbitrary"),
            vmem_limit_bytes=VMEM_LIMIT),
        name="mixer_front",
    )(x, gain, w_in, sc_conv, sc_out, pool_w, pool_scale, conf_conv, conf_b,
      ln_g, ln_b, conf_out, q_gain, k_gain, gsum)


def _diff_attn_kernel(shift_ref, q_ref, k_ref, v_ref, lam_ref, subln_ref, o_ref, *, lam_init):
    tq, tk = TQ, TK
    qi = pl.program_id(2)
    q = q_ref[0]
    lane = lax.broadcasted_iota(jnp.int32, (tq, 2 * HEAD_DIM), 1)
    zero = jnp.zeros_like(q)
    qs = jnp.concatenate([jnp.where(lane < HEAD_DIM, q, zero),
                          jnp.where(lane >= HEAD_DIM, q, zero)], axis=0)

    def scores(j, part, masked):
        start = pl.multiple_of(j * tq + part * tk, tk)
        k = k_ref[0, pl.ds(start, tk), :]
        v = v_ref[0, pl.ds(start, tk), :]
        s = lax.dot_general(qs, k, (((1,), (1,)), ((), ())), preferred_element_type=F32)
        if masked:
            row = lax.broadcasted_iota(jnp.int32, (2 * tq, tk), 0)
            col = lax.broadcasted_iota(jnp.int32, (2 * tq, tk), 1) + part * tk
            row = jnp.where(row >= tq, row - tq, row)
            s = jnp.where(col <= row, s, MASKED)
        return s, v

    def over_key_groups(step, init):
        def group(j, carry, masked):
            for part in range(tq // tk):
                carry = step(j, part, carry, masked)
            return carry
        carry = lax.fori_loop(0, qi, functools.partial(group, masked=False), init)
        return group(qi, carry, masked=True)

    def finish(o, l):
        o = o / l
        lam_v = lam_ref[...]
        lam = (jnp.exp(jnp.sum(lam_v[0:1] * lam_v[1:2], axis=-1, keepdims=True))
               - jnp.exp(jnp.sum(lam_v[2:3] * lam_v[3:4], axis=-1, keepdims=True)) + lam_init)
        o = o[:tq] - lam * o[tq:]
        o = _rms_norm_rows(o, subln_ref[...]) * (1.0 - lam_init)
        o_ref[0] = o.astype(BF16)

    shift = shift_ref[0]

    @pl.when(shift <= MAX_FIXED_SHIFT)
    def _():
        lane_k = lax.broadcasted_iota(jnp.int32, (tk, 2 * HEAD_DIM), 1)
        ones_col = jnp.where(lane_k == 0, 1.0, 0.0).astype(BF16)

        def step(j, part, acc, masked):
            s, v = scores(j, part, masked)
            p = jnp.exp2(s - shift).astype(BF16)
            v_aug = jnp.concatenate([v, ones_col], axis=1)
            return acc + jnp.dot(p, v_aug, preferred_element_type=F32)

        acc = over_key_groups(step, jnp.zeros((2 * tq, 4 * HEAD_DIM), F32))
        finish(acc[:, :2 * HEAD_DIM], acc[:, 2 * HEAD_DIM:2 * HEAD_DIM + 1])

    @pl.when(shift > MAX_FIXED_SHIFT)
    def _():
        def step(j, part, carry, masked):
            m_prev, l_prev, acc_prev = carry
            s, v = scores(j, part, masked)
            m_new = jnp.maximum(m_prev, jnp.max(s, axis=-1, keepdims=True))
            alpha = jnp.exp2(m_prev - m_new)
            p = jnp.exp2(s - m_new)
            l_new = alpha * l_prev + jnp.sum(p, axis=-1, keepdims=True)
            acc_new = alpha * acc_prev + jnp.dot(p.astype(BF16), v, preferred_element_type=F32)
            return m_new, l_new, acc_new

        init = (jnp.full((2 * tq, 1), MASKED, F32),
                jnp.zeros((2 * tq, 1), F32),
                jnp.zeros((2 * tq, 2 * HEAD_DIM), F32))
        _, l_fin, acc = over_key_groups(step, init)
        finish(acc, l_fin)


def _diff_attn(shift, q, k, v, lam_vecs, subln, lam_init):
    bsz, seq, _ = q.shape
    tq = TQ
    hw = 2 * HEAD_DIM
    grid = (bsz, N_HEADS, seq // tq)
    return pl.pallas_call(
        functools.partial(_diff_attn_kernel, lam_init=lam_init),
        grid=grid,
        in_specs=[pl.BlockSpec(memory_space=pltpu.SMEM),
                  pl.BlockSpec((1, tq, hw), lambda b, h, i: (b, i, h)),
                  pl.BlockSpec((1, seq, hw), lambda b, h, i: (b, 0, h)),
                  pl.BlockSpec((1, seq, hw), lambda b, h, i: (b, 0, h)),
                  pl.BlockSpec(lam_vecs.shape, lambda b, h, i: (0, 0)),
                  pl.BlockSpec(subln.shape, lambda b, h, i: (0, 0))],
        out_specs=pl.BlockSpec((1, tq, hw), lambda b, h, i: (b, i, h)),
        out_shape=jax.ShapeDtypeStruct((bsz, seq, D_V), BF16),
        compiler_params=pltpu.CompilerParams(
            dimension_semantics=("arbitrary", "arbitrary", "arbitrary"),
            vmem_limit_bytes=VMEM_LIMIT),
        name="diff_attn",
    )(shift, q, k, v, lam_vecs, subln)


def _merge_ffn_kernel(x_ref, o_ref, pm_ref, g3_ref, wd_ref, wo_ref, gain_ref, wup_ref,
                      fcw_ref, wdown_ref, out_ref, halo):
    tm = TM_FFN

    @pl.when(pl.program_id(1) == 0)
    def _():
        halo[...] = jnp.zeros_like(halo)

    y_diff = jnp.dot(o_ref[0], wd_ref[...], preferred_element_type=F32)
    merged = pm_ref[0] + g3_ref[0] * y_diff
    x1 = x_ref[0] + jnp.dot(merged.astype(BF16), wo_ref[...], preferred_element_type=F32)
    h2 = _rms_norm_rows(x1, gain_ref[...]).astype(BF16)

    def conv_proj(lo, width):
        u = jnp.dot(h2, wup_ref[:, lo:lo + width], preferred_element_type=F32)
        ext = jnp.concatenate([halo[:, lo:lo + width], u], axis=0)
        halo[:, lo:lo + width] = u[tm - FFN_HALO:]
        return _causal_conv3(ext, fcw_ref[:, lo:lo + width], FFN_HALO, tm)

    acc = x1
    lo = 0
    for width in FF_CHUNKS:
        g = conv_proj(lo, width)
        up = conv_proj(D_FF + lo, width)
        a = (g * _sigmoid(g) * up).astype(BF16)
        acc = acc + jnp.dot(a, wdown_ref[lo:lo + width, :], preferred_element_type=F32)
        lo += width
    out_ref[0] = acc


def _merge_ffn(x, o, pm, g3, w_diff, w_o, gain, w_up, ffn_conv, w_down):
    bsz, seq, _ = x.shape
    tm = TM_FFN
    grid = (bsz, seq // tm)
    tok = lambda width: pl.BlockSpec((1, tm, width), lambda b, i: (b, i, 0))
    in_specs = [tok(D_MODEL), tok(D_V), tok(D_MODEL), tok(D_MODEL)] + [
        _resident(a.shape) for a in (w_diff, w_o, gain, w_up, ffn_conv, w_down)]
    return pl.pallas_call(
        _merge_ffn_kernel,
        grid=grid,
        in_specs=in_specs,
        out_specs=tok(D_MODEL),
        out_shape=jax.ShapeDtypeStruct((bsz, seq, D_MODEL), F32),
        scratch_shapes=[pltpu.VMEM((FFN_HALO, 2 * D_FF), F32)],
        compiler_params=pltpu.CompilerParams(
            dimension_semantics=("arbitrary", "arbitrary"),
            vmem_limit_bytes=VMEM_LIMIT),
        name="merge_ffn",
    )(x, o, pm, g3, w_diff, w_o, gain, w_up, ffn_conv, w_down)


def kernel(x, mix_norm, w_in, sc_conv, sc_out, pool_w, pool_scale, conf_conv, conf_conv_b,
           conf_ln_g, conf_ln_b, conf_out, q_norm, k_norm, lambda_q1, lambda_k1, lambda_q2,
           lambda_k2, diff_subln, diff_out, w_o, ffn_norm, ffn_up, ffn_conv, ffn_down):
    depth = w_in.shape[0]
    assert sum(FF_CHUNKS) == D_FF and x.shape[1] % max(TM_FRONT, TM_FFN, TQ) == 0 and TQ % TK == 0
    assert 3 * D_MODEL <= SUBLANES * GATE_SLICE and (3 * D_MODEL) % GATE_SLICE == 0
    row = lambda a: a.reshape(1, -1).astype(F32)
    group = jnp.arange(D_QK) // HEAD_DIM
    gsum = (group[:, None] == group[None, :]).astype(BF16)
    for l in range(depth):
        lam_init = 0.8 - 0.6 * math.exp(-0.3 * l)
        q_gain = row(jnp.tile(q_norm[l], D_QK // HEAD_DIM)) * (HEAD_DIM ** -0.5 * LOG2_E)
        k_gain = row(jnp.tile(k_norm[l], D_QK // HEAD_DIM))
        shift = (jnp.max(jnp.abs(q_norm[l] * k_norm[l])) * (HEAD_DIM ** 0.5 * LOG2_E)).reshape(1)
        q, k, v, pm, g3 = _mixer_front(
            x, row(mix_norm[l]), w_in[l].astype(BF16), sc_conv[l], sc_out[l].astype(BF16),
            pool_w[l].astype(BF16), row(pool_scale[l]), conf_conv[l], row(conf_conv_b[l]),
            row(conf_ln_g[l]), row(conf_ln_b[l]), conf_out[l].astype(BF16), q_gain, k_gain, gsum)
        lam_vecs = jnp.stack([lambda_q1[l], lambda_k1[l], lambda_q2[l], lambda_k2[l]]).astype(F32)
        o = _diff_attn(shift.astype(F32), q, k, v, lam_vecs, row(diff_subln[l]), lam_init)
        x = _merge_ffn(x, o, pm, g3, diff_out[l].astype(BF16), w_o[l].astype(BF16),
                       row(ffn_norm[l]), ffn_up[l].astype(BF16), ffn_conv[l],
                       ffn_down[l].astype(BF16))
    return x
```

```python
import functools
import math

import jax
import jax.numpy as jnp
from jax import lax
from jax.experimental import pallas as pl
from jax.experimental.pallas import tpu as pltpu

D_MODEL = 1024
D_SC = 512
D_POOL = 512
POOL_WINDOWS = (2, 4, 8, 16)
POOL_GROUP = 128
POOL_OUT_GROUP = 256
D_CONF = 512
CONF_KERNEL = 31
N_HEADS = 4
HEAD_DIM = 64
D_QK = 512
D_V = 512
D_FF = 2816
EPS = 1e-6
MASKED = -1e30
LOG2_E = math.log2(math.e)
MAX_FIXED_SHIFT = 60.0

OFF_SC = 0
OFF_POOL = 3 * D_SC
OFF_CONF = OFF_POOL + D_POOL
OFF_Q = OFF_CONF + 2 * D_CONF
OFF_K = OFF_Q + D_QK
OFF_V = OFF_K + D_QK
OFF_GATE = OFF_V + D_V
N_IN = OFF_GATE + 4 * D_MODEL

SUBLANES = 8
LANES = 128
VMEM_LIMIT = 56 * 1024 * 1024

TM_FRONT = 512
TM_FFN = 512
TQ = 1024
TK = 512
FF_CHUNKS = (768, 768, 768, 512)

SC_HALO = 8
POOL_HALO = 16
CONF_HALO = 32
FFN_HALO = 8

F32 = jnp.float32
BF16 = jnp.bfloat16


def _rms_norm_rows(x, gain):
    ms = jnp.mean(x * x, axis=-1, keepdims=True)
    return x * lax.rsqrt(ms + EPS) * gain


def _sigmoid(x):
    return 0.5 * jnp.tanh(0.5 * x) + 0.5


def _lagged(ext, lag):
    if lag == 0:
        return ext
    return pltpu.roll(ext, lag, axis=0)


def _causal_conv3(ext, w, halo, rows):
    y = ext * w[2:3] + _lagged(ext, 1) * w[1:2] + _lagged(ext, 2) * w[0:1]
    return y[halo:halo + rows]


def _mixer_front_kernel(x_ref, gain_ref, w_ref, scw_ref, sco_ref, poolw_ref, pools_ref,
                        cfw_ref, cfb_ref, lng_ref, lnb_ref, cfo_ref, qg_ref, kg_ref, gsum_ref,
                        qT_out, k_out, vT_out, pm_out, g3_out,
                        sc_halo, pool_halo, conf_halo):
    tm = TM_FRONT

    @pl.when(pl.program_id(1) == 0)
    def _():
        sc_halo[...] = jnp.zeros_like(sc_halo)
        pool_halo[...] = jnp.zeros_like(pool_halo)
        conf_halo[...] = jnp.zeros_like(conf_halo)

    x = x_ref[0]
    h = _rms_norm_rows(x, gain_ref[...]).astype(BF16)

    def proj(lo, width):
        return jnp.dot(h, w_ref[:, lo:lo + width], preferred_element_type=F32)

    def gate(i):
        return _sigmoid(proj(OFF_GATE + i * D_MODEL, D_MODEL))

    z_sc = proj(OFF_SC, 3 * D_SC)
    z_cf = proj(OFF_CONF, 2 * D_CONF)
    xp = proj(OFF_POOL, D_POOL)

    u = z_sc[:, D_SC:2 * D_SC] * z_sc[:, 2 * D_SC:]
    ext = jnp.concatenate([sc_halo[...], u], axis=0)
    sc_halo[...] = u[tm - SC_HALO:]
    a_act = (z_sc[:, :D_SC] * _causal_conv3(ext, scw_ref[...], SC_HALO, tm)).astype(BF16)

    def qk_norm(lo, gain):
        t = proj(lo, D_QK)
        ss = jnp.dot((t * t).astype(BF16), gsum_ref[...], preferred_element_type=F32)
        return t * lax.rsqrt(ss * (1.0 / HEAD_DIM) + EPS) * gain

    q = qk_norm(OFF_Q, qg_ref[...])
    k_out[0] = qk_norm(OFF_K, kg_ref[...]).astype(BF16)
    v = proj(OFF_V, D_V)
    hw = 2 * HEAD_DIM
    for hd in range(N_HEADS):
        qT_out[0, hd] = q[:, hd * hw:(hd + 1) * hw].T.astype(BF16)
        for c in range(tm // TK):
            vT_out[0, hd, c] = v[c * TK:(c + 1) * TK, hd * hw:(hd + 1) * hw].T.astype(BF16)
    merged = gate(0) * jnp.dot(a_act, sco_ref[...], preferred_element_type=F32)

    ext = jnp.concatenate([pool_halo[...], xp], axis=0)
    pool_halo[...] = xp[tm - POOL_HALO:]
    t_pos = pl.program_id(1) * tm + lax.broadcasted_iota(jnp.int32, (tm, 1), 0)
    ys = []
    for g, win in enumerate(POOL_WINDOWS):
        s = ext[:, g * POOL_GROUP:(g + 1) * POOL_GROUP]
        span = 1
        while span < win:
            s = s + _lagged(s, span)
            span *= 2
        count = jnp.minimum(t_pos + 1, win).astype(F32)
        pooled = s[POOL_HALO:] / count - xp[:, g * POOL_GROUP:(g + 1) * POOL_GROUP]
        ys.append(jnp.dot(pooled.astype(BF16), poolw_ref[g], preferred_element_type=F32))
    y_pool = jnp.concatenate(ys, axis=1) * pools_ref[...]

    glu = z_cf[:, :D_CONF] * _sigmoid(z_cf[:, D_CONF:])
    ext = jnp.concatenate([conf_halo[...], glu], axis=0)
    conf_halo[...] = glu[tm - CONF_HALO:]
    cw = cfw_ref[...]
    acc = jnp.zeros((tm, D_CONF), F32) + cfb_ref[...]
    for r in range(SUBLANES):
        ext_r = _lagged(ext, r)
        for a in range(CONF_HALO // SUBLANES):
            lag = SUBLANES * a + r
            if lag >= CONF_KERNEL:
                continue
            tap = CONF_KERNEL - 1 - lag
            start = CONF_HALO - SUBLANES * a
            acc = acc + ext_r[start:start + tm] * cw[tap:tap + 1]
    mu = jnp.mean(acc, axis=-1, keepdims=True)
    cen = acc - mu
    var = jnp.mean(cen * cen, axis=-1, keepdims=True)
    y = cen * lax.rsqrt(var + EPS) * lng_ref[...] + lnb_ref[...]
    c_act = (y * _sigmoid(y)).astype(BF16)

    merged = merged + gate(1) * y_pool
    g3_out[0] = gate(3)
    y_conf = jnp.dot(c_act, cfo_ref[...], preferred_element_type=F32)
    pm_out[0] = merged + gate(2) * y_conf


def _resident(shape):
    zeros = (0,) * len(shape)
    return pl.BlockSpec(shape, lambda b, i: zeros, pipeline_mode=pl.Buffered(1))


def _mixer_front(x, gain, w_in, sc_conv, sc_out, pool_w, pool_scale, conf_conv, conf_b,
                 ln_g, ln_b, conf_out, q_gain, k_gain, gsum):
    bsz, seq, _ = x.shape
    tm = TM_FRONT
    grid = (bsz, seq // tm)
    tok = lambda width: pl.BlockSpec((1, tm, width), lambda b, i: (b, i, 0))
    operands = (x, gain, w_in, sc_conv, sc_out, pool_w, pool_scale, conf_conv,
                conf_b, ln_g, ln_b, conf_out, q_gain, k_gain, gsum)
    in_specs = [tok(D_MODEL)] + [_resident(a.shape) for a in operands[1:]]
    hw = 2 * HEAD_DIM
    out_shape = [jax.ShapeDtypeStruct((bsz, N_HEADS, hw, seq), BF16),
                 jax.ShapeDtypeStruct((bsz, seq, D_QK), BF16),
                 jax.ShapeDtypeStruct((bsz, N_HEADS, seq // TK, hw, TK), BF16),
                 jax.ShapeDtypeStruct((bsz, seq, D_MODEL), F32),
                 jax.ShapeDtypeStruct((bsz, seq, D_MODEL), F32)]
    out_specs = [pl.BlockSpec((1, N_HEADS, hw, tm), lambda b, i: (b, 0, 0, i)),
                 tok(D_QK),
                 pl.BlockSpec((1, N_HEADS, tm // TK, hw, TK), lambda b, i: (b, 0, i, 0, 0)),
                 tok(D_MODEL), tok(D_MODEL)]
    return pl.pallas_call(
        _mixer_front_kernel,
        grid=grid,
        in_specs=in_specs,
        out_specs=out_specs,
        out_shape=out_shape,
        scratch_shapes=[pltpu.VMEM((SC_HALO, D_SC), F32),
                        pltpu.VMEM((POOL_HALO, D_POOL), F32),
                        pltpu.VMEM((CONF_HALO, D_CONF), F32)],
        compiler_params=pltpu.CompilerParams(
            dimension_semantics=("arbitrary", "arbitrary"),
            vmem_limit_bytes=VMEM_LIMIT),
        name="mixer_front",
    )(*operands)


def _diff_attn_kernel(shift_ref, qT_ref, k_ref, vT_ref, lam_ref, subln_ref, o_ref, *, lam_init):
    tq, tk = TQ, TK
    qi = pl.program_id(2)
    qT = qT_ref[0, 0]
    feat = lax.broadcasted_iota(jnp.int32, (2 * HEAD_DIM, tq), 0)
    zero = jnp.zeros_like(qT)
    qsT = jnp.concatenate([jnp.where(feat < HEAD_DIM, qT, zero),
                           jnp.where(feat >= HEAD_DIM, qT, zero)], axis=1)

    def scores(j, part, masked):
        blk = j * (tq // tk) + part
        k = k_ref[0, pl.ds(pl.multiple_of(blk * tk, tk), tk), :]
        vT = vT_ref[0, 0, blk]
        sT = jnp.dot(k, qsT, preferred_element_type=F32)
        if masked:
            key = lax.broadcasted_iota(jnp.int32, (tk, 2 * tq), 0) + part * tk
            qry = lax.broadcasted_iota(jnp.int32, (tk, 2 * tq), 1)
            qry = jnp.where(qry >= tq, qry - tq, qry)
            sT = jnp.where(key <= qry, sT, MASKED)
        return sT, vT

    def over_key_groups(step, init):
        def group(j, carry, masked):
            for part in range(tq // tk):
                carry = step(j, part, carry, masked)
            return carry
        carry = lax.fori_loop(0, qi, functools.partial(group, masked=False), init)
        return group(qi, carry, masked=True)

    def key_sum(pT):
        return jnp.sum(pT.reshape(tk // SUBLANES, SUBLANES, pT.shape[1]), axis=0)

    def finish(oT, l):
        oT = oT / l
        lam_v = lam_ref[...]
        lam = (jnp.exp(jnp.sum(lam_v[0:1] * lam_v[1:2], axis=-1, keepdims=True))
               - jnp.exp(jnp.sum(lam_v[2:3] * lam_v[3:4], axis=-1, keepdims=True)) + lam_init)
        oT = oT[:, :tq] - lam * oT[:, tq:]
        ms = jnp.mean(oT * oT, axis=0, keepdims=True)
        oT = oT * lax.rsqrt(ms + EPS) * (subln_ref[...] * (1.0 - lam_init))
        o_ref[0] = oT.T.astype(BF16)

    shift = shift_ref[0]
    fixed_shift_ok = shift <= MAX_FIXED_SHIFT

    @pl.when(fixed_shift_ok)
    def _():
        def step(j, part, carry, masked):
            l_acc, accT = carry
            sT, vT = scores(j, part, masked)
            pT = jnp.exp2(sT - shift)
            accT = accT + jnp.dot(vT, pT.astype(BF16), preferred_element_type=F32)
            return l_acc + key_sum(pT), accT

        l_acc, accT = over_key_groups(step, (jnp.zeros((SUBLANES, 2 * tq), F32),
                                             jnp.zeros((2 * HEAD_DIM, 2 * tq), F32)))
        finish(accT, jnp.sum(l_acc, axis=0, keepdims=True))

    @pl.when(jnp.logical_not(fixed_shift_ok))
    def _():
        def step(j, part, carry, masked):
            m_prev, l_prev, acc_prev = carry
            sT, vT = scores(j, part, masked)
            m_new = jnp.maximum(m_prev, jnp.max(sT, axis=0, keepdims=True))
            alpha = jnp.exp2(m_prev - m_new)
            pT = jnp.exp2(sT - m_new)
            l_new = alpha * l_prev + jnp.sum(key_sum(pT), axis=0, keepdims=True)
            acc_new = alpha * acc_prev + jnp.dot(vT, pT.astype(BF16), preferred_element_type=F32)
            return m_new, l_new, acc_new

        init = (jnp.full((1, 2 * tq), MASKED, F32),
                jnp.zeros((1, 2 * tq), F32),
                jnp.zeros((2 * HEAD_DIM, 2 * tq), F32))
        _, l_fin, accT = over_key_groups(step, init)
        finish(accT, l_fin)


def _diff_attn(shift, qT, k, vT, lam_vecs, subln_col, lam_init):
    bsz, seq, _ = k.shape
    tq = TQ
    hw = 2 * HEAD_DIM
    grid = (bsz, N_HEADS, seq // tq)
    return pl.pallas_call(
        functools.partial(_diff_attn_kernel, lam_init=lam_init),
        grid=grid,
        in_specs=[pl.BlockSpec(memory_space=pltpu.SMEM),
                  pl.BlockSpec((1, 1, hw, tq), lambda b, h, i: (b, h, 0, i)),
                  pl.BlockSpec((1, seq, hw), lambda b, h, i: (b, 0, h)),
                  pl.BlockSpec((1, 1, seq // TK, hw, TK), lambda b, h, i: (b, h, 0, 0, 0)),
                  pl.BlockSpec(lam_vecs.shape, lambda b, h, i: (0, 0)),
                  pl.BlockSpec(subln_col.shape, lambda b, h, i: (0, 0))],
        out_specs=pl.BlockSpec((1, tq, hw), lambda b, h, i: (b, i, h)),
        out_shape=jax.ShapeDtypeStruct((bsz, seq, D_V), BF16),
        compiler_params=pltpu.CompilerParams(
            dimension_semantics=("arbitrary", "arbitrary", "arbitrary"),
            vmem_limit_bytes=VMEM_LIMIT),
        name="diff_attn",
    )(shift, qT, k, vT, lam_vecs, subln_col)


def _merge_ffn_kernel(x_ref, o_ref, pm_ref, g3_ref, wd_ref, wo_ref, gain_ref, wup_ref,
                      fcw_ref, wdown_ref, out_ref, halo):
    tm = TM_FFN

    @pl.when(pl.program_id(1) == 0)
    def _():
        halo[...] = jnp.zeros_like(halo)

    y_diff = jnp.dot(o_ref[0], wd_ref[...], preferred_element_type=F32)
    merged = pm_ref[0] + g3_ref[0] * y_diff
    x1 = x_ref[0] + jnp.dot(merged.astype(BF16), wo_ref[...], preferred_element_type=F32)
    h2 = _rms_norm_rows(x1, gain_ref[...]).astype(BF16)

    def conv_proj(lo, width):
        u = jnp.dot(h2, wup_ref[:, lo:lo + width], preferred_element_type=F32)
        ext = jnp.concatenate([halo[:, lo:lo + width], u], axis=0)
        halo[:, lo:lo + width] = u[tm - FFN_HALO:]
        return _causal_conv3(ext, fcw_ref[:, lo:lo + width], FFN_HALO, tm)

    acc = x1
    lo = 0
    for width in FF_CHUNKS:
        g = conv_proj(lo, width)
        up = conv_proj(D_FF + lo, width)
        a = (g * _sigmoid(g) * up).astype(BF16)
        acc = acc + jnp.dot(a, wdown_ref[lo:lo + width, :], preferred_element_type=F32)
        lo += width
    out_ref[0] = acc


def _merge_ffn(x, o, pm, g3, w_diff, w_o, gain, w_up, ffn_conv, w_down):
    bsz, seq, _ = x.shape
    tm = TM_FFN
    grid = (bsz, seq // tm)
    tok = lambda width: pl.BlockSpec((1, tm, width), lambda b, i: (b, i, 0))
    in_specs = [tok(D_MODEL), tok(D_V), tok(D_MODEL), tok(D_MODEL)] + [
        _resident(a.shape) for a in (w_diff, w_o, gain, w_up, ffn_conv, w_down)]
    return pl.pallas_call(
        _merge_ffn_kernel,
        grid=grid,
        in_specs=in_specs,
        out_specs=tok(D_MODEL),
        out_shape=jax.ShapeDtypeStruct((bsz, seq, D_MODEL), F32),
        scratch_shapes=[pltpu.VMEM((FFN_HALO, 2 * D_FF), F32)],
        compiler_params=pltpu.CompilerParams(
            dimension_semantics=("arbitrary", "arbitrary"),
            vmem_limit_bytes=VMEM_LIMIT),
        name="merge_ffn",
    )(x, o, pm, g3, w_diff, w_o, gain, w_up, ffn_conv, w_down)


def kernel(x, mix_norm, w_in, sc_conv, sc_out, pool_w, pool_scale, conf_conv, conf_conv_b,
           conf_ln_g, conf_ln_b, conf_out, q_norm, k_norm, lambda_q1, lambda_k1, lambda_q2,
           lambda_k2, diff_subln, diff_out, w_o, ffn_norm, ffn_up, ffn_conv, ffn_down):
    depth = w_in.shape[0]
    assert sum(FF_CHUNKS) == D_FF and x.shape[1] % max(TM_FRONT, TM_FFN, TQ) == 0 and TQ % TK == 0
    assert TM_FRONT % TK == 0
    row = lambda a: a.reshape(1, -1).astype(F32)
    group = jnp.arange(D_QK) // HEAD_DIM
    gsum = (group[:, None] == group[None, :]).astype(BF16)
    for l in range(depth):
        lam_init = 0.8 - 0.6 * math.exp(-0.3 * l)
        q_gain = row(jnp.tile(q_norm[l], D_QK // HEAD_DIM)) * (HEAD_DIM ** -0.5 * LOG2_E)
        k_gain = row(jnp.tile(k_norm[l], D_QK // HEAD_DIM))
        shift = (jnp.max(jnp.abs(q_norm[l] * k_norm[l])) * (HEAD_DIM ** 0.5 * LOG2_E)).reshape(1)
        qT, k, vT, pm, g3 = _mixer_front(
            x, row(mix_norm[l]), w_in[l].astype(BF16), sc_conv[l], sc_out[l].astype(BF16),
            pool_w[l].astype(BF16), row(pool_scale[l]), conf_conv[l], row(conf_conv_b[l]),
            row(conf_ln_g[l]), row(conf_ln_b[l]), conf_out[l].astype(BF16), q_gain, k_gain, gsum)
        lam_vecs = jnp.stack([lambda_q1[l], lambda_k1[l], lambda_q2[l], lambda_k2[l]]).astype(F32)
        subln_col = diff_subln[l].reshape(-1, 1).astype(F32)
        o = _diff_attn(shift.astype(F32), qT, k, vT, lam_vecs, subln_col, lam_init)
        x = _merge_ffn(x, o, pm, g3, diff_out[l].astype(BF16), w_o[l].astype(BF16),
                       row(ffn_norm[l]), ffn_up[l].astype(BF16), ffn_conv[l],
                       ffn_down[l].astype(BF16))
    return x
```

```python
import functools
import math

import jax
import jax.numpy as jnp
from jax import lax
from jax.experimental import pallas as pl
from jax.experimental.pallas import tpu as pltpu

D_MODEL = 1024
D_SC = 512
D_POOL = 512
POOL_WINDOWS = (2, 4, 8, 16)
POOL_GROUP = 128
POOL_OUT_GROUP = 256
D_CONF = 512
CONF_KERNEL = 31
N_HEADS = 4
HEAD_DIM = 64
D_QK = 512
D_V = 512
D_FF = 2816
EPS = 1e-6
MASKED = -1e30
LOG2_E = math.log2(math.e)
MAX_FIXED_SHIFT = 60.0

OFF_SC = 0
OFF_POOL = 3 * D_SC
OFF_CONF = OFF_POOL + D_POOL
OFF_Q = OFF_CONF + 2 * D_CONF
OFF_K = OFF_Q + D_QK
OFF_V = OFF_K + D_QK
OFF_GATE = OFF_V + D_V
N_IN = OFF_GATE + 4 * D_MODEL

SUBLANES = 8
LANES = 128
VMEM_LIMIT = 56 * 1024 * 1024

TM_FRONT = 512
TM_FFN = 512
TQ = 1024
TK = 512
GROUPS_PER_TRIP = 2
FF_CHUNKS = (1536, 1280)

SC_HALO = 8
POOL_HALO = 16
CONF_HALO = 32
FFN_HALO = 8

F32 = jnp.float32
BF16 = jnp.bfloat16


def _rms_norm_rows(x, gain):
    ms = jnp.mean(x * x, axis=-1, keepdims=True)
    return x * lax.rsqrt(ms + EPS) * gain


def _sigmoid(x):
    return 0.5 * jnp.tanh(0.5 * x) + 0.5


def _lagged(ext, lag):
    if lag == 0:
        return ext
    return pltpu.roll(ext, lag, axis=0)


def _causal_conv3(ext, w, halo, rows):
    y = ext * w[2:3] + _lagged(ext, 1) * w[1:2] + _lagged(ext, 2) * w[0:1]
    return y[halo:halo + rows]


def _mixer_front_kernel(x_ref, gain_ref, w_ref, scw_ref, sco_ref, poolw_ref, pools_ref,
                        cfw_ref, cfb_ref, lng_ref, lnb_ref, cfo_ref, qg_ref, kg_ref, gsum_ref,
                        qT_out, k_out, vT_out, pm_out, g3_out,
                        sc_halo, pool_halo, conf_halo):
    tm = TM_FRONT

    @pl.when(pl.program_id(1) == 0)
    def _():
        sc_halo[...] = jnp.zeros_like(sc_halo)
        pool_halo[...] = jnp.zeros_like(pool_halo)
        conf_halo[...] = jnp.zeros_like(conf_halo)

    x = x_ref[0]
    h = _rms_norm_rows(x, gain_ref[...]).astype(BF16)

    def proj(lo, width):
        return jnp.dot(h, w_ref[:, lo:lo + width], preferred_element_type=F32)

    def gate(i):
        return _sigmoid(proj(OFF_GATE + i * D_MODEL, D_MODEL))

    z_sc = proj(OFF_SC, 3 * D_SC)
    z_cf = proj(OFF_CONF, 2 * D_CONF)
    xp = proj(OFF_POOL, D_POOL)

    u = z_sc[:, D_SC:2 * D_SC] * z_sc[:, 2 * D_SC:]
    ext = jnp.concatenate([sc_halo[...], u], axis=0)
    sc_halo[...] = u[tm - SC_HALO:]
    a_act = (z_sc[:, :D_SC] * _causal_conv3(ext, scw_ref[...], SC_HALO, tm)).astype(BF16)

    def qk_norm(lo, gain):
        t = proj(lo, D_QK)
        ss = jnp.dot((t * t).astype(BF16), gsum_ref[...], preferred_element_type=F32)
        return t * lax.rsqrt(ss * (1.0 / HEAD_DIM) + EPS) * gain

    q = qk_norm(OFF_Q, qg_ref[...])
    k_out[0] = qk_norm(OFF_K, kg_ref[...]).astype(BF16)
    v = proj(OFF_V, D_V)
    hw = 2 * HEAD_DIM
    for hd in range(N_HEADS):
        qT_out[0, hd] = q[:, hd * hw:(hd + 1) * hw].T.astype(BF16)
        for c in range(tm // TK):
            vT_out[0, hd, c] = v[c * TK:(c + 1) * TK, hd * hw:(hd + 1) * hw].T.astype(BF16)
    merged = gate(0) * jnp.dot(a_act, sco_ref[...], preferred_element_type=F32)

    ext = jnp.concatenate([pool_halo[...], xp], axis=0)
    pool_halo[...] = xp[tm - POOL_HALO:]
    t_pos = pl.program_id(1) * tm + lax.broadcasted_iota(jnp.int32, (tm, 1), 0)
    ys = []
    for g, win in enumerate(POOL_WINDOWS):
        s = ext[:, g * POOL_GROUP:(g + 1) * POOL_GROUP]
        span = 1
        while span < win:
            s = s + _lagged(s, span)
            span *= 2
        count = jnp.minimum(t_pos + 1, win).astype(F32)
        pooled = s[POOL_HALO:] / count - xp[:, g * POOL_GROUP:(g + 1) * POOL_GROUP]
        ys.append(jnp.dot(pooled.astype(BF16), poolw_ref[g], preferred_element_type=F32))
    y_pool = jnp.concatenate(ys, axis=1) * pools_ref[...]

    glu = z_cf[:, :D_CONF] * _sigmoid(z_cf[:, D_CONF:])
    ext = jnp.concatenate([conf_halo[...], glu], axis=0)
    conf_halo[...] = glu[tm - CONF_HALO:]
    cw = cfw_ref[...]
    acc = jnp.zeros((tm, D_CONF), F32) + cfb_ref[...]
    for r in range(SUBLANES):
        ext_r = _lagged(ext, r)
        for a in range(CONF_HALO // SUBLANES):
            lag = SUBLANES * a + r
            if lag >= CONF_KERNEL:
                continue
            tap = CONF_KERNEL - 1 - lag
            start = CONF_HALO - SUBLANES * a
            acc = acc + ext_r[start:start + tm] * cw[tap:tap + 1]
    mu = jnp.mean(acc, axis=-1, keepdims=True)
    cen = acc - mu
    var = jnp.mean(cen * cen, axis=-1, keepdims=True)
    y = cen * lax.rsqrt(var + EPS) * lng_ref[...] + lnb_ref[...]
    c_act = (y * _sigmoid(y)).astype(BF16)

    merged = merged + gate(1) * y_pool
    g3_out[0] = gate(3)
    y_conf = jnp.dot(c_act, cfo_ref[...], preferred_element_type=F32)
    pm_out[0] = merged + gate(2) * y_conf


def _resident(shape):
    zeros = (0,) * len(shape)
    return pl.BlockSpec(shape, lambda b, i: zeros, pipeline_mode=pl.Buffered(1))


def _mixer_front(x, gain, w_in, sc_conv, sc_out, pool_w, pool_scale, conf_conv, conf_b,
                 ln_g, ln_b, conf_out, q_gain, k_gain, gsum):
    bsz, seq, _ = x.shape
    tm = TM_FRONT
    grid = (bsz, seq // tm)
    tok = lambda width: pl.BlockSpec((1, tm, width), lambda b, i: (b, i, 0))
    operands = (x, gain, w_in, sc_conv, sc_out, pool_w, pool_scale, conf_conv,
                conf_b, ln_g, ln_b, conf_out, q_gain, k_gain, gsum)
    in_specs = [tok(D_MODEL)] + [_resident(a.shape) for a in operands[1:]]
    hw = 2 * HEAD_DIM
    out_shape = [jax.ShapeDtypeStruct((bsz, N_HEADS, hw, seq), BF16),
                 jax.ShapeDtypeStruct((bsz, seq, D_QK), BF16),
                 jax.ShapeDtypeStruct((bsz, N_HEADS, seq // TK, hw, TK), BF16),
                 jax.ShapeDtypeStruct((bsz, seq, D_MODEL), F32),
                 jax.ShapeDtypeStruct((bsz, seq, D_MODEL), F32)]
    out_specs = [pl.BlockSpec((1, N_HEADS, hw, tm), lambda b, i: (b, 0, 0, i)),
                 tok(D_QK),
                 pl.BlockSpec((1, N_HEADS, tm // TK, hw, TK), lambda b, i: (b, 0, i, 0, 0)),
                 tok(D_MODEL), tok(D_MODEL)]
    return pl.pallas_call(
        _mixer_front_kernel,
        grid=grid,
        in_specs=in_specs,
        out_specs=out_specs,
        out_shape=out_shape,
        scratch_shapes=[pltpu.VMEM((SC_HALO, D_SC), F32),
                        pltpu.VMEM((POOL_HALO, D_POOL), F32),
                        pltpu.VMEM((CONF_HALO, D_CONF), F32)],
        compiler_params=pltpu.CompilerParams(
            dimension_semantics=("arbitrary", "arbitrary"),
            vmem_limit_bytes=VMEM_LIMIT),
        name="mixer_front",
    )(*operands)


def _diff_attn_kernel(shift_ref, qT_ref, k_ref, vT_ref, lam_ref, subln_ref, o_ref, *, lam_init):
    tq, tk = TQ, TK
    qi = pl.program_id(2)
    qT = qT_ref[0, 0]
    feat = lax.broadcasted_iota(jnp.int32, (2 * HEAD_DIM, tq), 0)
    zero = jnp.zeros_like(qT)
    qsT = jnp.concatenate([jnp.where(feat < HEAD_DIM, qT, zero),
                           jnp.where(feat >= HEAD_DIM, qT, zero)], axis=1)

    def scores(j, part, masked, lo=0):
        blk = j * (tq // tk) + part
        k = k_ref[0, pl.ds(pl.multiple_of(blk * tk, tk), tk), :]
        vT = vT_ref[0, 0, blk]
        w = tq - lo
        q_cols = qsT if lo == 0 else jnp.concatenate([qsT[:, lo:tq], qsT[:, tq + lo:]], axis=1)
        sT = jnp.dot(k, q_cols, preferred_element_type=F32)
        if masked:
            key = lax.broadcasted_iota(jnp.int32, (tk, 2 * w), 0) + part * tk
            qry = lax.broadcasted_iota(jnp.int32, (tk, 2 * w), 1)
            qry = jnp.where(qry >= w, qry - w, qry) + lo
            sT = jnp.where(key <= qry, sT, MASKED)
        return sT, vT

    def add_cols(full, delta, lo):
        if lo == 0:
            return full + delta
        w = tq - lo
        return jnp.concatenate([full[:, :lo], full[:, lo:tq] + delta[:, :w],
                                full[:, tq:tq + lo], full[:, tq + lo:] + delta[:, w:]], axis=1)

    def over_key_groups(step, init, groups_per_trip=1):
        def group(j, carry, masked):
            for part in range(tq // tk):
                carry = step(j, part, carry, masked)
            return carry

        def trip(t, carry):
            for g in range(groups_per_trip):
                carry = group(t * groups_per_trip + g, carry, masked=False)
            return carry

        n_trips = qi // groups_per_trip
        carry = lax.fori_loop(0, n_trips, trip, init)
        if groups_per_trip > 1:
            carry = lax.fori_loop(n_trips * groups_per_trip, qi,
                                  functools.partial(group, masked=False), carry)
        return group(qi, carry, masked=True)

    def key_sum(pT):
        return jnp.sum(pT.reshape(tk // SUBLANES, SUBLANES, pT.shape[1]), axis=0)

    def finish(oT, l):
        oT = oT / l
        lam_v = lam_ref[...]
        lam = (jnp.exp(jnp.sum(lam_v[0:1] * lam_v[1:2], axis=-1, keepdims=True))
               - jnp.exp(jnp.sum(lam_v[2:3] * lam_v[3:4], axis=-1, keepdims=True)) + lam_init)
        oT = oT[:, :tq] - lam * oT[:, tq:]
        ms = jnp.mean(oT * oT, axis=0, keepdims=True)
        oT = oT * lax.rsqrt(ms + EPS) * (subln_ref[...] * (1.0 - lam_init))
        o_ref[0] = oT.T.astype(BF16)

    shift = shift_ref[0]
    fixed_shift_ok = shift <= MAX_FIXED_SHIFT

    @pl.when(fixed_shift_ok)
    def _():
        def step(j, part, carry, masked):
            l_acc, accT = carry
            lo = part * tk if masked else 0
            sT, vT = scores(j, part, masked, lo)
            pT = jnp.exp2(sT - shift)
            oT = jnp.dot(vT, pT.astype(BF16), preferred_element_type=F32)
            return add_cols(l_acc, key_sum(pT), lo), add_cols(accT, oT, lo)

        init = (jnp.zeros((SUBLANES, 2 * tq), F32), jnp.zeros((2 * HEAD_DIM, 2 * tq), F32))
        l_acc, accT = over_key_groups(step, init, groups_per_trip=GROUPS_PER_TRIP)
        finish(accT, jnp.sum(l_acc, axis=0, keepdims=True))

    @pl.when(jnp.logical_not(fixed_shift_ok))
    def _():
        def step(j, part, carry, masked):
            m_prev, l_prev, acc_prev = carry
            sT, vT = scores(j, part, masked)
            m_new = jnp.maximum(m_prev, jnp.max(sT, axis=0, keepdims=True))
            alpha = jnp.exp2(m_prev - m_new)
            pT = jnp.exp2(sT - m_new)
            l_new = alpha * l_prev + jnp.sum(key_sum(pT), axis=0, keepdims=True)
            acc_new = alpha * acc_prev + jnp.dot(vT, pT.astype(BF16), preferred_element_type=F32)
            return m_new, l_new, acc_new

        init = (jnp.full((1, 2 * tq), MASKED, F32),
                jnp.zeros((1, 2 * tq), F32),
                jnp.zeros((2 * HEAD_DIM, 2 * tq), F32))
        _, l_fin, accT = over_key_groups(step, init)
        finish(accT, l_fin)


def _diff_attn(shift, qT, k, vT, lam_vecs, subln_col, lam_init):
    bsz, seq, _ = k.shape
    tq = TQ
    hw = 2 * HEAD_DIM
    grid = (bsz, N_HEADS, seq // tq)
    return pl.pallas_call(
        functools.partial(_diff_attn_kernel, lam_init=lam_init),
        grid=grid,
        in_specs=[pl.BlockSpec(memory_space=pltpu.SMEM),
                  pl.BlockSpec((1, 1, hw, tq), lambda b, h, i: (b, h, 0, i)),
                  pl.BlockSpec((1, seq, hw), lambda b, h, i: (b, 0, h)),
                  pl.BlockSpec((1, 1, seq // TK, hw, TK), lambda b, h, i: (b, h, 0, 0, 0)),
                  pl.BlockSpec(lam_vecs.shape, lambda b, h, i: (0, 0)),
                  pl.BlockSpec(subln_col.shape, lambda b, h, i: (0, 0))],
        out_specs=pl.BlockSpec((1, tq, hw), lambda b, h, i: (b, i, h)),
        out_shape=jax.ShapeDtypeStruct((bsz, seq, D_V), BF16),
        compiler_params=pltpu.CompilerParams(
            dimension_semantics=("arbitrary", "arbitrary", "arbitrary"),
            vmem_limit_bytes=VMEM_LIMIT),
        name="diff_attn",
    )(shift, qT, k, vT, lam_vecs, subln_col)


def _merge_ffn_kernel(x_ref, o_ref, pm_ref, g3_ref, wd_ref, wo_ref, gain_ref, wup_ref,
                      fcw_ref, wdown_ref, out_ref, halo):
    tm = TM_FFN

    @pl.when(pl.program_id(1) == 0)
    def _():
        halo[...] = jnp.zeros_like(halo)

    y_diff = jnp.dot(o_ref[0], wd_ref[...], preferred_element_type=F32)
    merged = pm_ref[0] + g3_ref[0] * y_diff
    x1 = x_ref[0] + jnp.dot(merged.astype(BF16), wo_ref[...], preferred_element_type=F32)
    h2 = _rms_norm_rows(x1, gain_ref[...]).astype(BF16)

    def conv_proj(lo, width):
        u = jnp.dot(h2, wup_ref[:, lo:lo + width], preferred_element_type=F32)
        ext = jnp.concatenate([halo[:, lo:lo + width], u], axis=0)
        halo[:, lo:lo + width] = u[tm - FFN_HALO:]
        return _causal_conv3(ext, fcw_ref[:, lo:lo + width], FFN_HALO, tm)

    acc = x1
    lo = 0
    for width in FF_CHUNKS:
        g = conv_proj(lo, width)
        up = conv_proj(D_FF + lo, width)
        a = (g * _sigmoid(g) * up).astype(BF16)
        acc = acc + jnp.dot(a, wdown_ref[lo:lo + width, :], preferred_element_type=F32)
        lo += width
    out_ref[0] = acc


def _merge_ffn(x, o, pm, g3, w_diff, w_o, gain, w_up, ffn_conv, w_down):
    bsz, seq, _ = x.shape
    tm = TM_FFN
    grid = (bsz, seq // tm)
    tok = lambda width: pl.BlockSpec((1, tm, width), lambda b, i: (b, i, 0))
    in_specs = [tok(D_MODEL), tok(D_V), tok(D_MODEL), tok(D_MODEL)] + [
        _resident(a.shape) for a in (w_diff, w_o, gain, w_up, ffn_conv, w_down)]
    return pl.pallas_call(
        _merge_ffn_kernel,
        grid=grid,
        in_specs=in_specs,
        out_specs=tok(D_MODEL),
        out_shape=jax.ShapeDtypeStruct((bsz, seq, D_MODEL), F32),
        scratch_shapes=[pltpu.VMEM((FFN_HALO, 2 * D_FF), F32)],
        compiler_params=pltpu.CompilerParams(
            dimension_semantics=("arbitrary", "arbitrary"),
            vmem_limit_bytes=VMEM_LIMIT),
        name="merge_ffn",
    )(x, o, pm, g3, w_diff, w_o, gain, w_up, ffn_conv, w_down)


def kernel(x, mix_norm, w_in, sc_conv, sc_out, pool_w, pool_scale, conf_conv, conf_conv_b,
           conf_ln_g, conf_ln_b, conf_out, q_norm, k_norm, lambda_q1, lambda_k1, lambda_q2,
           lambda_k2, diff_subln, diff_out, w_o, ffn_norm, ffn_up, ffn_conv, ffn_down):
    depth = w_in.shape[0]
    assert sum(FF_CHUNKS) == D_FF and x.shape[1] % max(TM_FRONT, TM_FFN, TQ) == 0 and TQ % TK == 0
    assert TM_FRONT % TK == 0
    row = lambda a: a.reshape(1, -1).astype(F32)
    group = jnp.arange(D_QK) // HEAD_DIM
    gsum = (group[:, None] == group[None, :]).astype(BF16)
    for l in range(depth):
        lam_init = 0.8 - 0.6 * math.exp(-0.3 * l)
        q_gain = row(jnp.tile(q_norm[l], D_QK // HEAD_DIM)) * (HEAD_DIM ** -0.5 * LOG2_E)
        k_gain = row(jnp.tile(k_norm[l], D_QK // HEAD_DIM))
        shift = (jnp.max(jnp.abs(q_norm[l] * k_norm[l])) * (HEAD_DIM ** 0.5 * LOG2_E)).reshape(1)
        qT, k, vT, pm, g3 = _mixer_front(
            x, row(mix_norm[l]), w_in[l].astype(BF16), sc_conv[l], sc_out[l].astype(BF16),
            pool_w[l].astype(BF16), row(pool_scale[l]), conf_conv[l], row(conf_conv_b[l]),
            row(conf_ln_g[l]), row(conf_ln_b[l]), conf_out[l].astype(BF16), q_gain, k_gain, gsum)
        lam_vecs = jnp.stack([lambda_q1[l], lambda_k1[l], lambda_q2[l], lambda_k2[l]]).astype(F32)
        subln_col = diff_subln[l].reshape(-1, 1).astype(F32)
        o = _diff_attn(shift.astype(F32), qT, k, vT, lam_vecs, subln_col, lam_init)
        x = _merge_ffn(x, o, pm, g3, diff_out[l].astype(BF16), w_o[l].astype(BF16),
                       row(ffn_norm[l]), ffn_up[l].astype(BF16), ffn_conv[l],
                       ffn_down[l].astype(BF16))
    return x
```

```python
import functools
import math

import jax
import jax.numpy as jnp
from jax import lax
from jax.experimental import pallas as pl
from jax.experimental.pallas import tpu as pltpu

D_MODEL = 1024
D_SC = 512
D_POOL = 512
POOL_WINDOWS = (2, 4, 8, 16)
POOL_GROUP = 128
POOL_OUT_GROUP = 256
D_CONF = 512
CONF_KERNEL = 31
N_HEADS = 4
HEAD_DIM = 64
D_QK = 512
D_V = 512
D_FF = 2816
EPS = 1e-6
MASKED = -1e30
LOG2_E = math.log2(math.e)
MAX_FIXED_SHIFT = 60.0

OFF_SC = 0
OFF_POOL = 3 * D_SC
OFF_CONF = OFF_POOL + D_POOL
OFF_Q = OFF_CONF + 2 * D_CONF
OFF_K = OFF_Q + D_QK
OFF_V = OFF_K + D_QK
OFF_GATE = OFF_V + D_V
N_IN = OFF_GATE + 4 * D_MODEL

SUBLANES = 8
LANES = 128
VMEM_LIMIT = 56 * 1024 * 1024

TM_FRONT = 512
TM_FFN = 512
TQ = 1024
TK = 1024
TK_DIAG = 512
GROUPS_PER_TRIP = 2
FF_CHUNKS = (1536, 1280)

SC_HALO = 8
POOL_HALO = 16
CONF_HALO = 32
FFN_HALO = 8

F32 = jnp.float32
BF16 = jnp.bfloat16


def _rms_norm_rows(x, gain):
    ms = jnp.mean(x * x, axis=-1, keepdims=True)
    return x * lax.rsqrt(ms + EPS) * gain


def _sigmoid(x):
    return 0.5 * jnp.tanh(0.5 * x) + 0.5


def _lagged(ext, lag):
    if lag == 0:
        return ext
    return pltpu.roll(ext, lag, axis=0)


def _causal_conv3(ext, w, halo, rows):
    y = ext * w[2:3] + _lagged(ext, 1) * w[1:2] + _lagged(ext, 2) * w[0:1]
    return y[halo:halo + rows]


def _mixer_front_kernel(x_ref, gain_ref, w_ref, scw_ref, sco_ref, poolw_ref, pools_ref,
                        cfw_ref, cfb_ref, lng_ref, lnb_ref, cfo_ref, qg_ref, kg_ref, gsum_ref,
                        qT_out, k_out, vT_out, pm_out, g3_out,
                        sc_halo, pool_halo, conf_halo):
    tm = TM_FRONT

    @pl.when(pl.program_id(1) == 0)
    def _():
        sc_halo[...] = jnp.zeros_like(sc_halo)
        pool_halo[...] = jnp.zeros_like(pool_halo)
        conf_halo[...] = jnp.zeros_like(conf_halo)

    x = x_ref[0]
    h = _rms_norm_rows(x, gain_ref[...]).astype(BF16)

    def proj(lo, width):
        return jnp.dot(h, w_ref[:, lo:lo + width], preferred_element_type=F32)

    def gate(i):
        return _sigmoid(proj(OFF_GATE + i * D_MODEL, D_MODEL))

    z_sc = proj(OFF_SC, 3 * D_SC)
    z_cf = proj(OFF_CONF, 2 * D_CONF)
    xp = proj(OFF_POOL, D_POOL)

    u = z_sc[:, D_SC:2 * D_SC] * z_sc[:, 2 * D_SC:]
    ext = jnp.concatenate([sc_halo[...], u], axis=0)
    sc_halo[...] = u[tm - SC_HALO:]
    a_act = (z_sc[:, :D_SC] * _causal_conv3(ext, scw_ref[...], SC_HALO, tm)).astype(BF16)

    def qk_norm(lo, gain):
        t = proj(lo, D_QK)
        ss = jnp.dot((t * t).astype(BF16), gsum_ref[...], preferred_element_type=F32)
        return t * lax.rsqrt(ss * (1.0 / HEAD_DIM) + EPS) * gain

    q = qk_norm(OFF_Q, qg_ref[...])
    k_out[0] = qk_norm(OFF_K, kg_ref[...]).astype(BF16)
    v = proj(OFF_V, D_V)
    hw = 2 * HEAD_DIM
    for hd in range(N_HEADS):
        qT_out[0, hd] = q[:, hd * hw:(hd + 1) * hw].T.astype(BF16)
        vw = min(tm, TK)
        for c in range(tm // vw):
            vT_out[0, hd, c] = v[c * vw:(c + 1) * vw, hd * hw:(hd + 1) * hw].T.astype(BF16)
    merged = gate(0) * jnp.dot(a_act, sco_ref[...], preferred_element_type=F32)

    ext = jnp.concatenate([pool_halo[...], xp], axis=0)
    pool_halo[...] = xp[tm - POOL_HALO:]
    t_pos = pl.program_id(1) * tm + lax.broadcasted_iota(jnp.int32, (tm, 1), 0)
    ys = []
    for g, win in enumerate(POOL_WINDOWS):
        s = ext[:, g * POOL_GROUP:(g + 1) * POOL_GROUP]
        span = 1
        while span < win:
            s = s + _lagged(s, span)
            span *= 2
        count = jnp.minimum(t_pos + 1, win).astype(F32)
        pooled = s[POOL_HALO:] / count - xp[:, g * POOL_GROUP:(g + 1) * POOL_GROUP]
        ys.append(jnp.dot(pooled.astype(BF16), poolw_ref[g], preferred_element_type=F32))
    y_pool = jnp.concatenate(ys, axis=1) * pools_ref[...]

    glu = z_cf[:, :D_CONF] * _sigmoid(z_cf[:, D_CONF:])
    ext = jnp.concatenate([conf_halo[...], glu], axis=0)
    conf_halo[...] = glu[tm - CONF_HALO:]
    cw = cfw_ref[...]
    acc = jnp.zeros((tm, D_CONF), F32) + cfb_ref[...]
    for r in range(SUBLANES):
        ext_r = _lagged(ext, r)
        for a in range(CONF_HALO // SUBLANES):
            lag = SUBLANES * a + r
            if lag >= CONF_KERNEL:
                continue
            tap = CONF_KERNEL - 1 - lag
            start = CONF_HALO - SUBLANES * a
            acc = acc + ext_r[start:start + tm] * cw[tap:tap + 1]
    mu = jnp.mean(acc, axis=-1, keepdims=True)
    cen = acc - mu
    var = jnp.mean(cen * cen, axis=-1, keepdims=True)
    y = cen * lax.rsqrt(var + EPS) * lng_ref[...] + lnb_ref[...]
    c_act = (y * _sigmoid(y)).astype(BF16)

    merged = merged + gate(1) * y_pool
    g3_out[0] = gate(3)
    y_conf = jnp.dot(c_act, cfo_ref[...], preferred_element_type=F32)
    pm_out[0] = merged + gate(2) * y_conf


def _resident(shape):
    zeros = (0,) * len(shape)
    return pl.BlockSpec(shape, lambda b, i: zeros, pipeline_mode=pl.Buffered(1))


def _mixer_front(x, gain, w_in, sc_conv, sc_out, pool_w, pool_scale, conf_conv, conf_b,
                 ln_g, ln_b, conf_out, q_gain, k_gain, gsum):
    bsz, seq, _ = x.shape
    tm = TM_FRONT
    grid = (bsz, seq // tm)
    tok = lambda width: pl.BlockSpec((1, tm, width), lambda b, i: (b, i, 0))
    operands = (x, gain, w_in, sc_conv, sc_out, pool_w, pool_scale, conf_conv,
                conf_b, ln_g, ln_b, conf_out, q_gain, k_gain, gsum)
    in_specs = [tok(D_MODEL)] + [_resident(a.shape) for a in operands[1:]]
    hw = 2 * HEAD_DIM
    out_shape = [jax.ShapeDtypeStruct((bsz, N_HEADS, hw, seq), BF16),
                 jax.ShapeDtypeStruct((bsz, seq, D_QK), BF16),
                 jax.ShapeDtypeStruct((bsz, N_HEADS, seq // TK, hw, TK), BF16),
                 jax.ShapeDtypeStruct((bsz, seq, D_MODEL), F32),
                 jax.ShapeDtypeStruct((bsz, seq, D_MODEL), F32)]
    out_specs = [pl.BlockSpec((1, N_HEADS, hw, tm), lambda b, i: (b, 0, 0, i)),
                 tok(D_QK),
                 (pl.BlockSpec((1, N_HEADS, tm // TK, hw, TK), lambda b, i: (b, 0, i, 0, 0))
                  if tm >= TK else
                  pl.BlockSpec((1, N_HEADS, 1, hw, tm),
                               lambda b, i: (b, 0, i // (TK // tm), 0, i % (TK // tm)))),
                 tok(D_MODEL), tok(D_MODEL)]
    return pl.pallas_call(
        _mixer_front_kernel,
        grid=grid,
        in_specs=in_specs,
        out_specs=out_specs,
        out_shape=out_shape,
        scratch_shapes=[pltpu.VMEM((SC_HALO, D_SC), F32),
                        pltpu.VMEM((POOL_HALO, D_POOL), F32),
                        pltpu.VMEM((CONF_HALO, D_CONF), F32)],
        compiler_params=pltpu.CompilerParams(
            dimension_semantics=("arbitrary", "arbitrary"),
            vmem_limit_bytes=VMEM_LIMIT),
        name="mixer_front",
    )(*operands)


def _diff_attn_kernel(shift_ref, qT_ref, k_ref, vT_ref, lam_ref, subln_ref, o_ref, *, lam_init):
    tq, tk = TQ, TK
    qi = pl.program_id(2)
    qT = qT_ref[0, 0]
    feat = lax.broadcasted_iota(jnp.int32, (2 * HEAD_DIM, tq), 0)
    zero = jnp.zeros_like(qT)
    qsT = jnp.concatenate([jnp.where(feat < HEAD_DIM, qT, zero),
                           jnp.where(feat >= HEAD_DIM, qT, zero)], axis=1)

    def block_rows(masked):
        return TK_DIAG if masked else tk

    def scores(j, part, masked, lo=0):
        kb = block_rows(masked)
        first = part * kb
        k = k_ref[0, pl.ds(pl.multiple_of(j * tq + first, kb), kb), :]
        vT = vT_ref[0, 0, j * (tq // tk) + first // tk]
        vT = vT[:, first % tk:first % tk + kb]
        w = tq - lo
        q_cols = qsT if lo == 0 else jnp.concatenate([qsT[:, lo:tq], qsT[:, tq + lo:]], axis=1)
        sT = jnp.dot(k, q_cols, preferred_element_type=F32)
        if masked:
            key = lax.broadcasted_iota(jnp.int32, (kb, 2 * w), 0) + first
            qry = lax.broadcasted_iota(jnp.int32, (kb, 2 * w), 1)
            qry = jnp.where(qry >= w, qry - w, qry) + lo
            sT = jnp.where(key <= qry, sT, MASKED)
        return sT, vT

    def add_cols(full, delta, lo):
        if lo == 0:
            return full + delta
        w = tq - lo
        return jnp.concatenate([full[:, :lo], full[:, lo:tq] + delta[:, :w],
                                full[:, tq:tq + lo], full[:, tq + lo:] + delta[:, w:]], axis=1)

    def over_key_groups(step, init, groups_per_trip=1):
        def group(j, carry, masked):
            for part in range(tq // block_rows(masked)):
                carry = step(j, part, carry, masked)
            return carry

        def trip(t, carry):
            for g in range(groups_per_trip):
                carry = group(t * groups_per_trip + g, carry, masked=False)
            return carry

        n_trips = qi // groups_per_trip
        carry = lax.fori_loop(0, n_trips, trip, init)
        if groups_per_trip > 1:
            carry = lax.fori_loop(n_trips * groups_per_trip, qi,
                                  functools.partial(group, masked=False), carry)
        return group(qi, carry, masked=True)

    def key_sum(pT):
        return jnp.sum(pT.reshape(pT.shape[0] // SUBLANES, SUBLANES, pT.shape[1]), axis=0)

    def finish(oT, l):
        oT = oT / l
        lam_v = lam_ref[...]
        lam = (jnp.exp(jnp.sum(lam_v[0:1] * lam_v[1:2], axis=-1, keepdims=True))
               - jnp.exp(jnp.sum(lam_v[2:3] * lam_v[3:4], axis=-1, keepdims=True)) + lam_init)
        oT = oT[:, :tq] - lam * oT[:, tq:]
        ms = jnp.mean(oT * oT, axis=0, keepdims=True)
        oT = oT * lax.rsqrt(ms + EPS) * (subln_ref[...] * (1.0 - lam_init))
        o_ref[0] = oT.T.astype(BF16)

    shift = shift_ref[0]
    fixed_shift_ok = shift <= MAX_FIXED_SHIFT

    @pl.when(fixed_shift_ok)
    def _():
        def step(j, part, carry, masked):
            l_acc, accT = carry
            lo = part * TK_DIAG if masked else 0
            sT, vT = scores(j, part, masked, lo)
            pT = jnp.exp2(sT - shift)
            oT = jnp.dot(vT, pT.astype(BF16), preferred_element_type=F32)
            return add_cols(l_acc, key_sum(pT), lo), add_cols(accT, oT, lo)

        init = (jnp.zeros((SUBLANES, 2 * tq), F32), jnp.zeros((2 * HEAD_DIM, 2 * tq), F32))
        l_acc, accT = over_key_groups(step, init, groups_per_trip=GROUPS_PER_TRIP)
        finish(accT, jnp.sum(l_acc, axis=0, keepdims=True))

    @pl.when(jnp.logical_not(fixed_shift_ok))
    def _():
        def step(j, part, carry, masked):
            m_prev, l_prev, acc_prev = carry
            sT, vT = scores(j, part, masked)
            m_new = jnp.maximum(m_prev, jnp.max(sT, axis=0, keepdims=True))
            alpha = jnp.exp2(m_prev - m_new)
            pT = jnp.exp2(sT - m_new)
            l_new = alpha * l_prev + jnp.sum(key_sum(pT), axis=0, keepdims=True)
            acc_new = alpha * acc_prev + jnp.dot(vT, pT.astype(BF16), preferred_element_type=F32)
            return m_new, l_new, acc_new

        init = (jnp.full((1, 2 * tq), MASKED, F32),
                jnp.zeros((1, 2 * tq), F32),
                jnp.zeros((2 * HEAD_DIM, 2 * tq), F32))
        _, l_fin, accT = over_key_groups(step, init)
        finish(accT, l_fin)


def _diff_attn(shift, qT, k, vT, lam_vecs, subln_col, lam_init):
    bsz, seq, _ = k.shape
    tq = TQ
    hw = 2 * HEAD_DIM
    grid = (bsz, N_HEADS, seq // tq)
    return pl.pallas_call(
        functools.partial(_diff_attn_kernel, lam_init=lam_init),
        grid=grid,
        in_specs=[pl.BlockSpec(memory_space=pltpu.SMEM),
                  pl.BlockSpec((1, 1, hw, tq), lambda b, h, i: (b, h, 0, i)),
                  pl.BlockSpec((1, seq, hw), lambda b, h, i: (b, 0, h)),
                  pl.BlockSpec((1, 1, seq // TK, hw, TK), lambda b, h, i: (b, h, 0, 0, 0)),
                  pl.BlockSpec(lam_vecs.shape, lambda b, h, i: (0, 0)),
                  pl.BlockSpec(subln_col.shape, lambda b, h, i: (0, 0))],
        out_specs=pl.BlockSpec((1, tq, hw), lambda b, h, i: (b, i, h)),
        out_shape=jax.ShapeDtypeStruct((bsz, seq, D_V), BF16),
        compiler_params=pltpu.CompilerParams(
            dimension_semantics=("arbitrary", "arbitrary", "arbitrary"),
            vmem_limit_bytes=VMEM_LIMIT),
        name="diff_attn",
    )(shift, qT, k, vT, lam_vecs, subln_col)


def _merge_ffn_kernel(x_ref, o_ref, pm_ref, g3_ref, wd_ref, wo_ref, gain_ref, wup_ref,
                      fcw_ref, wdown_ref, out_ref, halo):
    tm = TM_FFN

    @pl.when(pl.program_id(1) == 0)
    def _():
        halo[...] = jnp.zeros_like(halo)

    y_diff = jnp.dot(o_ref[0], wd_ref[...], preferred_element_type=F32)
    merged = pm_ref[0] + g3_ref[0] * y_diff
    x1 = x_ref[0] + jnp.dot(merged.astype(BF16), wo_ref[...], preferred_element_type=F32)
    h2 = _rms_norm_rows(x1, gain_ref[...]).astype(BF16)

    def conv_proj(lo, width):
        u = jnp.dot(h2, wup_ref[:, lo:lo + width], preferred_element_type=F32)
        ext = jnp.concatenate([halo[:, lo:lo + width], u], axis=0)
        halo[:, lo:lo + width] = u[tm - FFN_HALO:]
        return _causal_conv3(ext, fcw_ref[:, lo:lo + width], FFN_HALO, tm)

    acc = x1
    lo = 0
    for width in FF_CHUNKS:
        g = conv_proj(lo, width)
        up = conv_proj(D_FF + lo, width)
        a = (g * _sigmoid(g) * up).astype(BF16)
        acc = acc + jnp.dot(a, wdown_ref[lo:lo + width, :], preferred_element_type=F32)
        lo += width
    out_ref[0] = acc


def _merge_ffn(x, o, pm, g3, w_diff, w_o, gain, w_up, ffn_conv, w_down):
    bsz, seq, _ = x.shape
    tm = TM_FFN
    grid = (bsz, seq // tm)
    tok = lambda width: pl.BlockSpec((1, tm, width), lambda b, i: (b, i, 0))
    in_specs = [tok(D_MODEL), tok(D_V), tok(D_MODEL), tok(D_MODEL)] + [
        _resident(a.shape) for a in (w_diff, w_o, gain, w_up, ffn_conv, w_down)]
    return pl.pallas_call(
        _merge_ffn_kernel,
        grid=grid,
        in_specs=in_specs,
        out_specs=tok(D_MODEL),
        out_shape=jax.ShapeDtypeStruct((bsz, seq, D_MODEL), F32),
        scratch_shapes=[pltpu.VMEM((FFN_HALO, 2 * D_FF), F32)],
        compiler_params=pltpu.CompilerParams(
            dimension_semantics=("arbitrary", "arbitrary"),
            vmem_limit_bytes=VMEM_LIMIT),
        name="merge_ffn",
    )(x, o, pm, g3, w_diff, w_o, gain, w_up, ffn_conv, w_down)


def kernel(x, mix_norm, w_in, sc_conv, sc_out, pool_w, pool_scale, conf_conv, conf_conv_b,
           conf_ln_g, conf_ln_b, conf_out, q_norm, k_norm, lambda_q1, lambda_k1, lambda_q2,
           lambda_k2, diff_subln, diff_out, w_o, ffn_norm, ffn_up, ffn_conv, ffn_down):
    depth = w_in.shape[0]
    assert sum(FF_CHUNKS) == D_FF and x.shape[1] % max(TM_FRONT, TM_FFN, TQ) == 0 and TQ % TK == 0
    assert TM_FRONT % TK == 0 or TK % TM_FRONT == 0
    assert TK % TK_DIAG == 0 and TQ % TK_DIAG == 0
    row = lambda a: a.reshape(1, -1).astype(F32)
    group = jnp.arange(D_QK) // HEAD_DIM
    gsum = (group[:, None] == group[None, :]).astype(BF16)
    for l in range(depth):
        lam_init = 0.8 - 0.6 * math.exp(-0.3 * l)
        q_gain = row(jnp.tile(q_norm[l], D_QK // HEAD_DIM)) * (HEAD_DIM ** -0.5 * LOG2_E)
        k_gain = row(jnp.tile(k_norm[l], D_QK // HEAD_DIM))
        shift = (jnp.max(jnp.abs(q_norm[l] * k_norm[l])) * (HEAD_DIM ** 0.5 * LOG2_E)).reshape(1)
        qT, k, vT, pm, g3 = _mixer_front(
            x, row(mix_norm[l]), w_in[l].astype(BF16), sc_conv[l], sc_out[l].astype(BF16),
            pool_w[l].astype(BF16), row(pool_scale[l]), conf_conv[l], row(conf_conv_b[l]),
            row(conf_ln_g[l]), row(conf_ln_b[l]), conf_out[l].astype(BF16), q_gain, k_gain, gsum)
        lam_vecs = jnp.stack([lambda_q1[l], lambda_k1[l], lambda_q2[l], lambda_k2[l]]).astype(F32)
        subln_col = diff_subln[l].reshape(-1, 1).astype(F32)
        o = _diff_attn(shift.astype(F32), qT, k, vT, lam_vecs, subln_col, lam_init)
        x = _merge_ffn(x, o, pm, g3, diff_out[l].astype(BF16), w_o[l].astype(BF16),
                       row(ffn_norm[l]), ffn_up[l].astype(BF16), ffn_conv[l],
                       ffn_down[l].astype(BF16))
    return x
```

```python
import functools
import math

import jax
import jax.numpy as jnp
from jax import lax
from jax.experimental import pallas as pl
from jax.experimental.pallas import tpu as pltpu

D_MODEL = 1024
D_SC = 512
D_POOL = 512
POOL_WINDOWS = (2, 4, 8, 16)
POOL_GROUP = 128
POOL_OUT_GROUP = 256
D_CONF = 512
CONF_KERNEL = 31
N_HEADS = 4
HEAD_DIM = 64
D_QK = 512
D_V = 512
D_FF = 2816
EPS = 1e-6
MASKED = -1e30
LOG2_E = math.log2(math.e)
MAX_FIXED_SHIFT = 60.0

OFF_SC = 0
OFF_POOL = 3 * D_SC
OFF_CONF = OFF_POOL + D_POOL
OFF_Q = OFF_CONF + 2 * D_CONF
OFF_K = OFF_Q + D_QK
OFF_V = OFF_K + D_QK
OFF_GATE = OFF_V + D_V
N_IN = OFF_GATE + 4 * D_MODEL

SUBLANES = 8
LANES = 128
VMEM_LIMIT = 56 * 1024 * 1024

TM_FRONT = 512
TM_FFN = 512
TQ = 1024
TK = 1024
TK_DIAG = 512
GROUPS_PER_TRIP = 4
FF_CHUNKS = (1536, 1280)

SC_HALO = 8
POOL_HALO = 16
CONF_HALO = 32
FFN_HALO = 8

F32 = jnp.float32
BF16 = jnp.bfloat16


def _rms_norm_rows(x, gain):
    ms = jnp.mean(x * x, axis=-1, keepdims=True)
    return x * lax.rsqrt(ms + EPS) * gain


def _sigmoid(x):
    return 0.5 * jnp.tanh(0.5 * x) + 0.5


def _lagged(ext, lag):
    if lag == 0:
        return ext
    return pltpu.roll(ext, lag, axis=0)


def _causal_conv3(ext, w, halo, rows):
    y = ext * w[2:3] + _lagged(ext, 1) * w[1:2] + _lagged(ext, 2) * w[0:1]
    return y[halo:halo + rows]


def _mixer_front_kernel(x_ref, gain_ref, w_ref, scw_ref, sco_ref, poolw_ref, pools_ref,
                        cfw_ref, cfb_ref, lng_ref, lnb_ref, cfo_ref, qg_ref, kg_ref, gsum_ref,
                        qT_out, k_out, vT_out, pm_out, g3_out,
                        sc_halo, pool_halo, conf_halo):
    tm = TM_FRONT

    @pl.when(pl.program_id(1) == 0)
    def _():
        sc_halo[...] = jnp.zeros_like(sc_halo)
        pool_halo[...] = jnp.zeros_like(pool_halo)
        conf_halo[...] = jnp.zeros_like(conf_halo)

    x = x_ref[0]
    h = _rms_norm_rows(x, gain_ref[...]).astype(BF16)

    def proj(lo, width):
        return jnp.dot(h, w_ref[:, lo:lo + width], preferred_element_type=F32)

    def gate(i):
        return _sigmoid(proj(OFF_GATE + i * D_MODEL, D_MODEL))

    z_sc = proj(OFF_SC, 3 * D_SC)
    z_cf = proj(OFF_CONF, 2 * D_CONF)
    xp = proj(OFF_POOL, D_POOL)

    u = z_sc[:, D_SC:2 * D_SC] * z_sc[:, 2 * D_SC:]
    ext = jnp.concatenate([sc_halo[...], u], axis=0)
    sc_halo[...] = u[tm - SC_HALO:]
    a_act = (z_sc[:, :D_SC] * _causal_conv3(ext, scw_ref[...], SC_HALO, tm)).astype(BF16)

    def qk_norm(lo, gain):
        t = proj(lo, D_QK)
        ss = jnp.dot((t * t).astype(BF16), gsum_ref[...], preferred_element_type=F32)
        return t * lax.rsqrt(ss * (1.0 / HEAD_DIM) + EPS) * gain

    q = qk_norm(OFF_Q, qg_ref[...])
    k_out[0] = qk_norm(OFF_K, kg_ref[...]).astype(BF16)
    v = proj(OFF_V, D_V)
    hw = 2 * HEAD_DIM
    for hd in range(N_HEADS):
        qT_out[0, hd] = q[:, hd * hw:(hd + 1) * hw].T.astype(BF16)
        vw = min(tm, TK)
        for c in range(tm // vw):
            vT_out[0, hd, c] = v[c * vw:(c + 1) * vw, hd * hw:(hd + 1) * hw].T.astype(BF16)
    merged = gate(0) * jnp.dot(a_act, sco_ref[...], preferred_element_type=F32)

    ext = jnp.concatenate([pool_halo[...], xp], axis=0)
    pool_halo[...] = xp[tm - POOL_HALO:]
    t_pos = pl.program_id(1) * tm + lax.broadcasted_iota(jnp.int32, (tm, 1), 0)
    ys = []
    for g, win in enumerate(POOL_WINDOWS):
        s = ext[:, g * POOL_GROUP:(g + 1) * POOL_GROUP]
        span = 1
        while span < win:
            s = s + _lagged(s, span)
            span *= 2
        count = jnp.minimum(t_pos + 1, win).astype(F32)
        pooled = s[POOL_HALO:] / count - xp[:, g * POOL_GROUP:(g + 1) * POOL_GROUP]
        ys.append(jnp.dot(pooled.astype(BF16), poolw_ref[g], preferred_element_type=F32))
    y_pool = jnp.concatenate(ys, axis=1) * pools_ref[...]

    glu = z_cf[:, :D_CONF] * _sigmoid(z_cf[:, D_CONF:])
    ext = jnp.concatenate([conf_halo[...], glu], axis=0)
    conf_halo[...] = glu[tm - CONF_HALO:]
    cw = cfw_ref[...]
    acc = jnp.zeros((tm, D_CONF), F32) + cfb_ref[...]
    for r in range(SUBLANES):
        ext_r = _lagged(ext, r)
        for a in range(CONF_HALO // SUBLANES):
            lag = SUBLANES * a + r
            if lag >= CONF_KERNEL:
                continue
            tap = CONF_KERNEL - 1 - lag
            start = CONF_HALO - SUBLANES * a
            acc = acc + ext_r[start:start + tm] * cw[tap:tap + 1]
    mu = jnp.mean(acc, axis=-1, keepdims=True)
    cen = acc - mu
    var = jnp.mean(cen * cen, axis=-1, keepdims=True)
    y = cen * lax.rsqrt(var + EPS) * lng_ref[...] + lnb_ref[...]
    c_act = (y * _sigmoid(y)).astype(BF16)

    merged = merged + gate(1) * y_pool
    g3_out[0] = gate(3)
    y_conf = jnp.dot(c_act, cfo_ref[...], preferred_element_type=F32)
    pm_out[0] = merged + gate(2) * y_conf


def _resident(shape):
    zeros = (0,) * len(shape)
    return pl.BlockSpec(shape, lambda b, i: zeros, pipeline_mode=pl.Buffered(1))


def _mixer_front(x, gain, w_in, sc_conv, sc_out, pool_w, pool_scale, conf_conv, conf_b,
                 ln_g, ln_b, conf_out, q_gain, k_gain, gsum):
    bsz, seq, _ = x.shape
    tm = TM_FRONT
    grid = (bsz, seq // tm)
    tok = lambda width: pl.BlockSpec((1, tm, width), lambda b, i: (b, i, 0))
    operands = (x, gain, w_in, sc_conv, sc_out, pool_w, pool_scale, conf_conv,
                conf_b, ln_g, ln_b, conf_out, q_gain, k_gain, gsum)
    in_specs = [tok(D_MODEL)] + [_resident(a.shape) for a in operands[1:]]
    hw = 2 * HEAD_DIM
    out_shape = [jax.ShapeDtypeStruct((bsz, N_HEADS, hw, seq), BF16),
                 jax.ShapeDtypeStruct((bsz, seq, D_QK), BF16),
                 jax.ShapeDtypeStruct((bsz, N_HEADS, seq // TK, hw, TK), BF16),
                 jax.ShapeDtypeStruct((bsz, seq, D_MODEL), F32),
                 jax.ShapeDtypeStruct((bsz, seq, D_MODEL), F32)]
    out_specs = [pl.BlockSpec((1, N_HEADS, hw, tm), lambda b, i: (b, 0, 0, i)),
                 tok(D_QK),
                 (pl.BlockSpec((1, N_HEADS, tm // TK, hw, TK), lambda b, i: (b, 0, i, 0, 0))
                  if tm >= TK else
                  pl.BlockSpec((1, N_HEADS, 1, hw, tm),
                               lambda b, i: (b, 0, i // (TK // tm), 0, i % (TK // tm)))),
                 tok(D_MODEL), tok(D_MODEL)]
    return pl.pallas_call(
        _mixer_front_kernel,
        grid=grid,
        in_specs=in_specs,
        out_specs=out_specs,
        out_shape=out_shape,
        scratch_shapes=[pltpu.VMEM((SC_HALO, D_SC), F32),
                        pltpu.VMEM((POOL_HALO, D_POOL), F32),
                        pltpu.VMEM((CONF_HALO, D_CONF), F32)],
        compiler_params=pltpu.CompilerParams(
            dimension_semantics=("arbitrary", "arbitrary"),
            vmem_limit_bytes=VMEM_LIMIT),
        name="mixer_front",
    )(*operands)


def _diff_attn_kernel(shift_ref, qT_ref, k_ref, vT_ref, lam_ref, subln_ref, o_ref, *, lam_init):
    tq, tk = TQ, TK
    qi = pl.program_id(2)
    qT = qT_ref[0, 0]
    feat = lax.broadcasted_iota(jnp.int32, (2 * HEAD_DIM, tq), 0)
    zero = jnp.zeros_like(qT)
    qsT = jnp.concatenate([jnp.where(feat < HEAD_DIM, qT, zero),
                           jnp.where(feat >= HEAD_DIM, qT, zero)], axis=1)

    def block_rows(masked):
        return TK_DIAG if masked else tk

    def scores(j, part, masked, lo=0):
        kb = block_rows(masked)
        first = part * kb
        k = k_ref[0, pl.ds(pl.multiple_of(j * tq + first, kb), kb), :]
        vT = vT_ref[0, 0, j * (tq // tk) + first // tk]
        vT = vT[:, first % tk:first % tk + kb]
        w = tq - lo
        q_cols = qsT if lo == 0 else jnp.concatenate([qsT[:, lo:tq], qsT[:, tq + lo:]], axis=1)
        sT = jnp.dot(k, q_cols, preferred_element_type=F32)
        if masked:
            key = lax.broadcasted_iota(jnp.int32, (kb, 2 * w), 0) + first
            qry = lax.broadcasted_iota(jnp.int32, (kb, 2 * w), 1)
            qry = jnp.where(qry >= w, qry - w, qry) + lo
            sT = jnp.where(key <= qry, sT, MASKED)
        return sT, vT

    def add_cols(full, delta, lo):
        if lo == 0:
            return full + delta
        w = tq - lo
        return jnp.concatenate([full[:, :lo], full[:, lo:tq] + delta[:, :w],
                                full[:, tq:tq + lo], full[:, tq + lo:] + delta[:, w:]], axis=1)

    def over_key_groups(step, init, groups_per_trip=1):
        def group(j, carry, masked):
            for part in range(tq // block_rows(masked)):
                carry = step(j, part, carry, masked)
            return carry

        def trip(t, carry):
            for g in range(groups_per_trip):
                carry = group(t * groups_per_trip + g, carry, masked=False)
            return carry

        n_trips = qi // groups_per_trip
        carry = lax.fori_loop(0, n_trips, trip, init)
        if groups_per_trip > 1:
            carry = lax.fori_loop(n_trips * groups_per_trip, qi,
                                  functools.partial(group, masked=False), carry)
        return group(qi, carry, masked=True)

    def key_sum(pT):
        return jnp.sum(pT.reshape(pT.shape[0] // SUBLANES, SUBLANES, pT.shape[1]), axis=0)

    def finish(oT, l):
        oT = oT / l
        lam_v = lam_ref[...]
        lam = (jnp.exp(jnp.sum(lam_v[0:1] * lam_v[1:2], axis=-1, keepdims=True))
               - jnp.exp(jnp.sum(lam_v[2:3] * lam_v[3:4], axis=-1, keepdims=True)) + lam_init)
        oT = oT[:, :tq] - lam * oT[:, tq:]
        ms = jnp.mean(oT * oT, axis=0, keepdims=True)
        oT = oT * lax.rsqrt(ms + EPS) * (subln_ref[...] * (1.0 - lam_init))
        o_ref[0] = oT.T.astype(BF16)

    shift = shift_ref[0]
    fixed_shift_ok = shift <= MAX_FIXED_SHIFT

    @pl.when(fixed_shift_ok)
    def _():
        def step(j, part, carry, masked):
            l_acc, accT = carry
            lo = part * TK_DIAG if masked else 0
            sT, vT = scores(j, part, masked, lo)
            pT = jnp.exp2(sT - shift)
            oT = jnp.dot(vT, pT.astype(BF16), preferred_element_type=F32)
            return add_cols(l_acc, key_sum(pT), lo), add_cols(accT, oT, lo)

        init = (jnp.zeros((SUBLANES, 2 * tq), F32), jnp.zeros((2 * HEAD_DIM, 2 * tq), F32))
        l_acc, accT = over_key_groups(step, init, groups_per_trip=GROUPS_PER_TRIP)
        finish(accT, jnp.sum(l_acc, axis=0, keepdims=True))

    @pl.when(jnp.logical_not(fixed_shift_ok))
    def _():
        def step(j, part, carry, masked):
            m_prev, l_prev, acc_prev = carry
            sT, vT = scores(j, part, masked)
            m_new = jnp.maximum(m_prev, jnp.max(sT, axis=0, keepdims=True))
            alpha = jnp.exp2(m_prev - m_new)
            pT = jnp.exp2(sT - m_new)
            l_new = alpha * l_prev + jnp.sum(key_sum(pT), axis=0, keepdims=True)
            acc_new = alpha * acc_prev + jnp.dot(vT, pT.astype(BF16), preferred_element_type=F32)
            return m_new, l_new, acc_new

        init = (jnp.full((1, 2 * tq), MASKED, F32),
                jnp.zeros((1, 2 * tq), F32),
                jnp.zeros((2 * HEAD_DIM, 2 * tq), F32))
        _, l_fin, accT = over_key_groups(step, init)
        finish(accT, l_fin)


def _diff_attn(shift, qT, k, vT, lam_vecs, subln_col, lam_init):
    bsz, seq, _ = k.shape
    tq = TQ
    hw = 2 * HEAD_DIM
    grid = (bsz, N_HEADS, seq // tq)
    return pl.pallas_call(
        functools.partial(_diff_attn_kernel, lam_init=lam_init),
        grid=grid,
        in_specs=[pl.BlockSpec(memory_space=pltpu.SMEM),
                  pl.BlockSpec((1, 1, hw, tq), lambda b, h, i: (b, h, 0, i)),
                  pl.BlockSpec((1, seq, hw), lambda b, h, i: (b, 0, h)),
                  pl.BlockSpec((1, 1, seq // TK, hw, TK), lambda b, h, i: (b, h, 0, 0, 0)),
                  pl.BlockSpec(lam_vecs.shape, lambda b, h, i: (0, 0)),
                  pl.BlockSpec(subln_col.shape, lambda b, h, i: (0, 0))],
        out_specs=pl.BlockSpec((1, tq, hw), lambda b, h, i: (b, i, h)),
        out_shape=jax.ShapeDtypeStruct((bsz, seq, D_V), BF16),
        compiler_params=pltpu.CompilerParams(
            dimension_semantics=("arbitrary", "arbitrary", "arbitrary"),
            vmem_limit_bytes=VMEM_LIMIT),
        name="diff_attn",
    )(shift, qT, k, vT, lam_vecs, subln_col)


def _merge_ffn_kernel(x_ref, o_ref, pm_ref, g3_ref, wd_ref, wo_ref, gain_ref, wup_ref,
                      fcw_ref, wdown_ref, out_ref, halo):
    tm = TM_FFN

    @pl.when(pl.program_id(1) == 0)
    def _():
        halo[...] = jnp.zeros_like(halo)

    y_diff = jnp.dot(o_ref[0], wd_ref[...], preferred_element_type=F32)
    merged = pm_ref[0] + g3_ref[0] * y_diff
    x1 = x_ref[0] + jnp.dot(merged.astype(BF16), wo_ref[...], preferred_element_type=F32)
    h2 = _rms_norm_rows(x1, gain_ref[...]).astype(BF16)

    def conv_proj(lo, width):
        u = jnp.dot(h2, wup_ref[:, lo:lo + width], preferred_element_type=F32)
        ext = jnp.concatenate([halo[:, lo:lo + width], u], axis=0)
        halo[:, lo:lo + width] = u[tm - FFN_HALO:]
        return _causal_conv3(ext, fcw_ref[:, lo:lo + width], FFN_HALO, tm)

    acc = x1
    lo = 0
    for width in FF_CHUNKS:
        g = conv_proj(lo, width)
        up = conv_proj(D_FF + lo, width)
        a = (g * _sigmoid(g) * up).astype(BF16)
        acc = acc + jnp.dot(a, wdown_ref[lo:lo + width, :], preferred_element_type=F32)
        lo += width
    out_ref[0] = acc


def _merge_ffn(x, o, pm, g3, w_diff, w_o, gain, w_up, ffn_conv, w_down):
    bsz, seq, _ = x.shape
    tm = TM_FFN
    grid = (bsz, seq // tm)
    tok = lambda width: pl.BlockSpec((1, tm, width), lambda b, i: (b, i, 0))
    in_specs = [tok(D_MODEL), tok(D_V), tok(D_MODEL), tok(D_MODEL)] + [
        _resident(a.shape) for a in (w_diff, w_o, gain, w_up, ffn_conv, w_down)]
    return pl.pallas_call(
        _merge_ffn_kernel,
        grid=grid,
        in_specs=in_specs,
        out_specs=tok(D_MODEL),
        out_shape=jax.ShapeDtypeStruct((bsz, seq, D_MODEL), F32),
        scratch_shapes=[pltpu.VMEM((FFN_HALO, 2 * D_FF), F32)],
        compiler_params=pltpu.CompilerParams(
            dimension_semantics=("arbitrary", "arbitrary"),
            vmem_limit_bytes=VMEM_LIMIT),
        name="merge_ffn",
    )(x, o, pm, g3, w_diff, w_o, gain, w_up, ffn_conv, w_down)


def kernel(x, mix_norm, w_in, sc_conv, sc_out, pool_w, pool_scale, conf_conv, conf_conv_b,
           conf_ln_g, conf_ln_b, conf_out, q_norm, k_norm, lambda_q1, lambda_k1, lambda_q2,
           lambda_k2, diff_subln, diff_out, w_o, ffn_norm, ffn_up, ffn_conv, ffn_down):
    depth = w_in.shape[0]
    assert sum(FF_CHUNKS) == D_FF and x.shape[1] % max(TM_FRONT, TM_FFN, TQ) == 0 and TQ % TK == 0
    assert TM_FRONT % TK == 0 or TK % TM_FRONT == 0
    assert TK % TK_DIAG == 0 and TQ % TK_DIAG == 0
    row = lambda a: a.reshape(1, -1).astype(F32)
    group = jnp.arange(D_QK) // HEAD_DIM
    gsum = (group[:, None] == group[None, :]).astype(BF16)
    for l in range(depth):
        lam_init = 0.8 - 0.6 * math.exp(-0.3 * l)
        q_gain = row(jnp.tile(q_norm[l], D_QK // HEAD_DIM)) * (HEAD_DIM ** -0.5 * LOG2_E)
        k_gain = row(jnp.tile(k_norm[l], D_QK // HEAD_DIM))
        shift = (jnp.max(jnp.abs(q_norm[l] * k_norm[l])) * (HEAD_DIM ** 0.5 * LOG2_E)).reshape(1)
        qT, k, vT, pm, g3 = _mixer_front(
            x, row(mix_norm[l]), w_in[l].astype(BF16), sc_conv[l], sc_out[l].astype(BF16),
            pool_w[l].astype(BF16), row(pool_scale[l]), conf_conv[l], row(conf_conv_b[l]),
            row(conf_ln_g[l]), row(conf_ln_b[l]), conf_out[l].astype(BF16), q_gain, k_gain, gsum)
        lam_vecs = jnp.stack([lambda_q1[l], lambda_k1[l], lambda_q2[l], lambda_k2[l]]).astype(F32)
        subln_col = diff_subln[l].reshape(-1, 1).astype(F32)
        o = _diff_attn(shift.astype(F32), qT, k, vT, lam_vecs, subln_col, lam_init)
        x = _merge_ffn(x, o, pm, g3, diff_out[l].astype(BF16), w_o[l].astype(BF16),
                       row(ffn_norm[l]), ffn_up[l].astype(BF16), ffn_conv[l],
                       ffn_down[l].astype(BF16))
    return x
```
